```python
import math
import jax, jax.numpy as jnp
from jax import lax
import numpy as np

D_MODEL = 1024
BATCH = 8
SEQ = 4096
DEPTH = 1

GRID_W = 64
ATT_HEADS = 8
ATT_KV_HEADS = 2
ATT_HEAD_DIM = 64
ATT_WIDTH = ATT_HEADS * ATT_HEAD_DIM
ATT_KV_WIDTH = ATT_KV_HEADS * ATT_HEAD_DIM
Q_BLOCK = 128
ROPE_THETA = 10000.0
HG_HEADS = 4
HG_DK = 128
HG_DV = 128
HG_WIDTH_K = HG_HEADS * HG_DK
HG_WIDTH_V = HG_HEADS * HG_DV
HG_CHUNK = 64
N_BRANCH = 2
IN_SPLITS = (ATT_WIDTH, ATT_KV_WIDTH, ATT_KV_WIDTH,
             HG_WIDTH_K, HG_WIDTH_K, HG_WIDTH_K, HG_WIDTH_V, HG_WIDTH_V,
             N_BRANCH * D_MODEL)
IN_WIDTH = sum(IN_SPLITS)
PEER_HEADS = 8
PEER_NKEYS = 128
PEER_N_EXPERTS = PEER_NKEYS * PEER_NKEYS
PEER_DKEY = 256
PEER_TOPK = 16
PEER_TOKEN_BLOCK = 128
PLE_DIM = 256
EPS = 1e-6

kernel_name = "hybrid_gqa_hgrn2_peer_encoder_block"


def rmsnorm(x, g):
    xf = x.astype(jnp.float32)
    y = xf * lax.rsqrt(jnp.mean(xf * xf, axis=-1, keepdims=True) + EPS)
    return (y * g.astype(jnp.float32)).astype(x.dtype)


def axial_rope_tables(seq_len):
    rows = seq_len // GRID_W
    row = jnp.repeat(jnp.arange(rows), GRID_W).astype(jnp.float32)
    col = jnp.tile(jnp.arange(GRID_W), rows).astype(jnp.float32)
    axis_dim = ATT_HEAD_DIM // 2
    rot_half = axis_dim // 2
    inv = ROPE_THETA ** (-jnp.arange(rot_half, dtype=jnp.float32) / rot_half)
    ang = jnp.stack([row[:, None] * inv, col[:, None] * inv], axis=1)
    return jnp.cos(ang), jnp.sin(ang)


def apply_axial_rope(x, cos, sin):
    B, S, H, hd = x.shape
    xr = x.reshape(B, S, H, 2, 2, hd // 4).astype(jnp.float32)
    x1, x2 = xr[..., 0, :], xr[..., 1, :]
    c = cos[None, :, None]
    s = sin[None, :, None]
    out = jnp.stack([x1 * c - x2 * s, x1 * s + x2 * c], axis=-2)
    return out.reshape(B, S, H, hd).astype(x.dtype)


def gqa_attention(q, k, v):
    B, S = q.shape[:2]
    G = ATT_HEADS // ATT_KV_HEADS
    nb = S // Q_BLOCK
    scale = ATT_HEAD_DIM ** -0.5
    qb = q.reshape(B, nb, Q_BLOCK, ATT_KV_HEADS, G, ATT_HEAD_DIM).transpose(1, 0, 2, 3, 4, 5)

    def one_block(qblk):
        s = jnp.einsum('bqkgd,bskd->bkgqs', qblk, k, preferred_element_type=jnp.float32) * scale
        pr = jax.nn.softmax(s, axis=-1)
        return jnp.einsum('bkgqs,bskd->bqkgd', pr.astype(v.dtype), v)

    o = lax.map(one_block, qb)
    return o.transpose(1, 0, 2, 3, 4, 5).reshape(B, S, ATT_WIDTH)


def gla_chunk_scan(q, k, v, log_f):
    B, S, H, dk = q.shape
    dv = v.shape[-1]
    C = HG_CHUNK
    nc = S // C

    def to_chunks(a):
        return a.reshape(B, nc, C, H, a.shape[-1]).transpose(1, 0, 3, 2, 4)

    qc, kc, vc, gc = to_chunks(q), to_chunks(k), to_chunks(v), to_chunks(log_f)
    lower = jnp.tril(jnp.ones((C, C), dtype=bool))[:, :, None]

    def step(state, inp):
        qi, ki, vi, gi = inp
        b = jnp.cumsum(gi.astype(jnp.float32), axis=2)
        diff = b[:, :, :, None, :] - b[:, :, None, :, :]
        decay = jnp.exp(jnp.where(lower, diff, -jnp.inf))
        A = jnp.einsum('bhtd,bhsd,bhtsd->bhts', qi.astype(jnp.float32), ki.astype(jnp.float32), decay)
        o_intra = jnp.einsum('bhts,bhse->bhte', A, vi.astype(jnp.float32))
        o_inter = jnp.einsum('bhtd,bhde->bhte', qi.astype(jnp.float32) * jnp.exp(b), state)
        b_last = b[:, :, -1:, :]
        k_dec = ki.astype(jnp.float32) * jnp.exp(b_last - b)
        new_state = state * jnp.exp(b_last[:, :, 0, :])[..., None] + \
            jnp.einsum('bhsd,bhse->bhde', k_dec, vi.astype(jnp.float32))
        return new_state, o_intra + o_inter

    state0 = jnp.zeros((B, H, dk, dv), jnp.float32)
    _, o = lax.scan(step, state0, (qc, kc, vc, gc))
    return o.transpose(1, 0, 3, 2, 4).reshape(B, S, H, dv)


def hgrn2_bidirectional(hq, hf_f, hf_b, hi, hg, lb, out_norm):
    B, S = hq.shape[:2]
    q = jax.nn.silu(hq).reshape(B, S, HG_HEADS, HG_DK)
    i_in = hi.reshape(B, S, HG_HEADS, HG_DV)

    def gates(fpre, lbd):
        f = lbd + (1.0 - lbd) * jax.nn.sigmoid(fpre.astype(jnp.float32))
        f = f.reshape(B, S, HG_HEADS, HG_DK)
        return 1.0 - f, jnp.log(f)

    k_f, lf_f = gates(hf_f, lb[0])
    k_b, lf_b = gates(hf_b, lb[1])
    o_fwd = gla_chunk_scan(q, k_f, i_in, lf_f)
    flip = lambda a: jnp.flip(a, axis=1)
    o_bwd = flip(gla_chunk_scan(flip(q), flip(k_b), flip(i_in), flip(lf_b)))
    o = rmsnorm(o_fwd + o_bwd, out_norm)
    o = o * jax.nn.silu(hg.astype(jnp.float32).reshape(B, S, HG_HEADS, HG_DV))
    return o.reshape(B, S, HG_WIDTH_V).astype(hq.dtype)


def peer_ffn(h, w_q, sub_keys, u, v):
    B, S, D = h.shape
    T = B * S
    K = PEER_TOPK
    ht = h.reshape(T, D)
    q = (ht @ w_q).reshape(T, PEER_HEADS, 2, PEER_DKEY // 2)
    s = jnp.einsum('thpc,hpnc->thpn', q, sub_keys, preferred_element_type=jnp.float32)
    v1, i1 = lax.top_k(s[:, :, 0], K)
    v2, i2 = lax.top_k(s[:, :, 1], K)
    cand = (v1[..., :, None] + v2[..., None, :]).reshape(T, PEER_HEADS, K * K)
    cand_idx = (i1[..., :, None] * PEER_NKEYS + i2[..., None, :]).reshape(T, PEER_HEADS, K * K)
    top_s, pos = lax.top_k(cand, K)
    idx = jnp.take_along_axis(cand_idx, pos, axis=-1)
    g = jax.nn.softmax(top_s, axis=-1)
    nb = T // PEER_TOKEN_BLOCK

    def one_block(args):
        xb, ib, gb = args
        ue = jnp.take(u, ib, axis=0)
        a = jnp.einsum('thkd,td->thk', ue, xb, preferred_element_type=jnp.float32)
        w = jax.nn.gelu(a) * gb
        ve = jnp.take(v, ib, axis=0)
        return jnp.einsum('thk,thkd->td', w.astype(ve.dtype), ve)

    out = lax.map(one_block, (ht.reshape(nb, PEER_TOKEN_BLOCK, D),
                              idx.reshape(nb, PEER_TOKEN_BLOCK, PEER_HEADS, K),
                              g.reshape(nb, PEER_TOKEN_BLOCK, PEER_HEADS, K)))
    return out.reshape(B, S, D).astype(h.dtype)


def setup_inputs(seed: int = 0) -> dict:
    key = jax.random.key(seed)
    ks = jax.random.split(key, 24)
    f32 = jnp.float32
    nrm = lambda k, shape, scale: jax.random.normal(k, shape, f32) * scale
    gain = lambda k, shape: 1.0 + 0.02 * jax.random.normal(k, shape, f32)
    return {
        "x": nrm(ks[0], (BATCH, SEQ, D_MODEL), 1.0),
        "p": nrm(ks[1], (DEPTH, BATCH, SEQ, PLE_DIM), 1.0),
        "norm_mix": gain(ks[2], (DEPTH, D_MODEL)),
        "w_in": nrm(ks[3], (DEPTH, D_MODEL, IN_WIDTH), D_MODEL ** -0.5),
        "q_norm": gain(ks[4], (DEPTH, ATT_HEAD_DIM)),
        "k_norm": gain(ks[5], (DEPTH, ATT_HEAD_DIM)),
        "hg_lb_raw": nrm(ks[6], (DEPTH + 1, 2, HG_WIDTH_K), 0.1),
        "hg_out_norm": gain(ks[7], (DEPTH, HG_DV)),
        "w_up_att": nrm(ks[8], (DEPTH, ATT_WIDTH, D_MODEL), ATT_WIDTH ** -0.5),
        "w_up_hg": nrm(ks[9], (DEPTH, HG_WIDTH_V, D_MODEL), HG_WIDTH_V ** -0.5),
        "w_out": nrm(ks[10], (DEPTH, D_MODEL, D_MODEL), D_MODEL ** -0.5),
        "norm_ffn": gain(ks[11], (DEPTH, D_MODEL)),
        "peer_wq": nrm(ks[12], (DEPTH, D_MODEL, PEER_HEADS * PEER_DKEY), D_MODEL ** -0.5),
        "peer_subkeys": nrm(ks[13], (DEPTH, PEER_HEADS, 2, PEER_NKEYS, PEER_DKEY // 2), (PEER_DKEY // 2) ** -0.5),
        "peer_u": nrm(ks[14], (DEPTH, PEER_N_EXPERTS, D_MODEL), D_MODEL ** -0.5),
        "peer_v": nrm(ks[15], (DEPTH, PEER_N_EXPERTS, D_MODEL), 0.3),
        "norm_ple": gain(ks[16], (DEPTH, D_MODEL)),
        "ple_gate": nrm(ks[17], (DEPTH, D_MODEL, D_MODEL), D_MODEL ** -0.5),
        "ple_proj": nrm(ks[18], (DEPTH, PLE_DIM, D_MODEL), PLE_DIM ** -0.5),
    }


def reference(x, p, norm_mix, w_in, q_norm, k_norm, hg_lb_raw, hg_out_norm, w_up_att, w_up_hg,
              w_out, norm_ffn, peer_wq, peer_subkeys, peer_u, peer_v, norm_ple, ple_gate, ple_proj):
    B, S, D = x.shape
    cos, sin = axial_rope_tables(S)
    lb_all = jnp.cumsum(jax.nn.softmax(hg_lb_raw.astype(jnp.float32), axis=0), axis=0)
    offsets = [int(o) for o in np.cumsum(IN_SPLITS)[:-1]]
    for i in range(DEPTH):
        h = rmsnorm(x, norm_mix[i])
        proj = h @ w_in[i]
        aq, ak, av, hq, hf_f, hf_b, hi, hg, gate_pre = jnp.split(proj, offsets, axis=-1)
        q = rmsnorm(aq.reshape(B, S, ATT_HEADS, ATT_HEAD_DIM), q_norm[i])
        k = rmsnorm(ak.reshape(B, S, ATT_KV_HEADS, ATT_HEAD_DIM), k_norm[i])
        v = av.reshape(B, S, ATT_KV_HEADS, ATT_HEAD_DIM)
        q = apply_axial_rope(q, cos, sin)
        k = apply_axial_rope(k, cos, sin)
        y_att = gqa_attention(q, k, v) @ w_up_att[i]
        y_hg = hgrn2_bidirectional(hq, hf_f, hf_b, hi, hg, lb_all[i], hg_out_norm[i]) @ w_up_hg[i]
        gts = jax.nn.sigmoid(gate_pre.astype(jnp.float32)).reshape(B, S, N_BRANCH, D)
        merged = (gts[:, :, 0] * y_att + gts[:, :, 1] * y_hg).astype(x.dtype)
        x = x + merged @ w_out[i]
        x = x + peer_ffn(rmsnorm(x, norm_ffn[i]), peer_wq[i], peer_subkeys[i], peer_u[i], peer_v[i])
        g_ple = jax.nn.sigmoid(rmsnorm(x, norm_ple[i]) @ ple_gate[i])
        x = x + g_ple * (p[i] @ ple_proj[i])
    return x
```

```python
import functools
import math

import jax
import jax.numpy as jnp
import numpy as np
from jax import lax
from jax.experimental import pallas as pl
from jax.experimental.pallas import tpu as pltpu

F32 = jnp.float32
BF16 = jnp.bfloat16
I32 = jnp.int32
U32 = jnp.uint32

D_MODEL = 1024
GRID_W = 64
ATT_HEADS = 8
ATT_KV_HEADS = 2
ATT_GROUP = ATT_HEADS // ATT_KV_HEADS
ATT_HEAD_DIM = 64
ATT_WIDTH = ATT_HEADS * ATT_HEAD_DIM
ATT_KV_WIDTH = ATT_KV_HEADS * ATT_HEAD_DIM
ROPE_THETA = 10000.0
HG_HEADS = 4
HG_DK = 128
HG_DV = 128
HG_WIDTH = HG_HEADS * HG_DK
HG_CHUNK = 64
HG_SUB = 16
N_BRANCH = 2
PEER_HEADS = 8
PEER_NKEYS = 128
PEER_DKEY = 256
PEER_TOPK = 16
PEER_PAIRS = PEER_HEADS * PEER_TOPK
PEER_N_EXPERTS = PEER_NKEYS * PEER_NKEYS
PLE_DIM = 256
EPS = 1e-6

LANES = 128
VMEM_LIMIT_BYTES = 56 * 1024 * 1024

TM_PROJ = 256
TQ_ATT = 256
TM_PEER = 128
PEER_SUB = 8


def _cparams(n_axes):
    return pltpu.CompilerParams(
        dimension_semantics=("arbitrary",) * n_axes,
        vmem_limit_bytes=VMEM_LIMIT_BYTES,
    )


def _const_spec(shape):
    nd = len(shape)
    return pl.BlockSpec(shape, lambda *_: (0,) * nd)


def _dot(a, b):
    return jnp.dot(a, b, preferred_element_type=F32)


def _dot_nt(a, b):
    return lax.dot_general(a, b, (((1,), (1,)), ((), ())), preferred_element_type=F32)


def _dot_tn(a, b):
    return lax.dot_general(a, b, (((0,), (0,)), ((), ())), preferred_element_type=F32)


def _split3(x):
    hi = x.astype(BF16)
    r = x - hi.astype(F32)
    mid = r.astype(BF16)
    lo = (r - mid.astype(F32)).astype(BF16)
    return hi, mid, lo


def _rms(x, gain):
    ms = jnp.mean(x * x, axis=-1, keepdims=True)
    return x * lax.rsqrt(ms + EPS) * gain


def _head_rms_rope(a, gain, m_blk, cos, sin_signed, first_half):
    sq = a * a
    hi = sq.astype(BF16)
    lo = (sq - hi.astype(F32)).astype(BF16)
    ms = _dot(hi, m_blk) + _dot(lo, m_blk)
    y = a * lax.rsqrt(ms + EPS) * gain
    w = y.shape[-1]
    nxt = pltpu.roll(y, w - HG_SUB, axis=1)
    prv = pltpu.roll(y, HG_SUB, axis=1)
    partner = jnp.where(first_half, nxt, prv)
    return y * cos + partner * sin_signed


def _inproj_kernel(x_ref, nm_ref, wq_ref, wk_ref, wv_ref, whq_ref, wff_ref, wfb_ref, whi_ref,
                   whg_ref, wg_ref, qg_ref, kg_ref, cos_ref, sin_ref, m_ref, lbraw_ref,
                   q_out, k_out, v_out, hq_out, kf_out, lff_out, kb_out, lfb_out, hi_out,
                   hg_out, g_out):
    h = _rms(x_ref[...], nm_ref[...]).astype(BF16)

    def proj(w_ref):
        return _dot(h, w_ref[...])

    cos = cos_ref[...]
    sin = sin_ref[...]
    lane = lax.broadcasted_iota(I32, (1, LANES), 1)
    first_half = (lane % 32) < 16

    def tile_lanes(t, reps):
        return jnp.concatenate([t] * reps, axis=1) if reps > 1 else t

    rq = ATT_WIDTH // LANES
    q = _head_rms_rope(proj(wq_ref), qg_ref[...], m_ref[...], tile_lanes(cos, rq),
                       tile_lanes(sin, rq), tile_lanes(first_half, rq))
    q_out[...] = (q * (ATT_HEAD_DIM ** -0.5)).astype(BF16)
    k = _head_rms_rope(proj(wk_ref), kg_ref[...], m_ref[0:ATT_KV_WIDTH, 0:ATT_KV_WIDTH],
                       cos, sin, first_half)
    k_out[...] = k.astype(BF16)
    v_out[...] = proj(wv_ref).astype(BF16)

    hq = proj(whq_ref)
    hq_out[...] = hq * jax.nn.sigmoid(hq)

    r = lbraw_ref[...]
    for d, (w_ref, k_ref, lf_ref) in enumerate(((wff_ref, kf_out, lff_out),
                                                 (wfb_ref, kb_out, lfb_out))):
        a0 = r[d:d + 1]
        a1 = r[2 + d:3 + d]
        mx = jnp.maximum(a0, a1)
        e0 = jnp.exp(a0 - mx)
        e1 = jnp.exp(a1 - mx)
        lb = e0 / (e0 + e1)
        f = lb + (1.0 - lb) * jax.nn.sigmoid(proj(w_ref))
        k_ref[...] = 1.0 - f
        lf_ref[...] = jnp.log(f)

    hi_out[...] = proj(whi_ref)
    hg = proj(whg_ref)
    hg_out[...] = hg * jax.nn.sigmoid(hg)
    g_out[...] = jax.nn.sigmoid(proj(wg_ref))


def _rope_tables(seq_len):
    lane = np.arange(LANES)
    axis = (lane % ATT_HEAD_DIM) // 32
    j = lane % 16
    sign = np.where((lane % 32) < 16, -1.0, 1.0).astype(np.float32)
    rot_half = ATT_HEAD_DIM // 4
    inv = ROPE_THETA ** (-jnp.arange(rot_half, dtype=F32) / rot_half)
    t = jnp.arange(seq_len)
    pos = jnp.stack([(t // GRID_W).astype(F32), (t % GRID_W).astype(F32)], axis=1)
    ang = pos[:, axis] * inv[j][None, :]
    return jnp.cos(ang), jnp.sin(ang) * sign[None, :]


def _inproj(x2, seq_len, norm_mix, w_in, q_norm, k_norm, lb_raw):
    t_total = x2.shape[0]
    tm = TM_PROJ
    n_tiles = t_total // tm
    tiles_per_seq = seq_len // tm
    splits = (ATT_WIDTH, ATT_KV_WIDTH, ATT_KV_WIDTH, HG_WIDTH, HG_WIDTH, HG_WIDTH, HG_WIDTH,
              HG_WIDTH, N_BRANCH * D_MODEL)
    offs = np.cumsum((0,) + splits)
    w_bf = w_in.astype(BF16)
    ws = [w_bf[:, offs[i]:offs[i + 1]] for i in range(len(splits))]
    cos, sin = _rope_tables(seq_len)
    grp = np.arange(ATT_WIDTH) // ATT_HEAD_DIM
    m_blk = jnp.asarray((grp[:, None] == grp[None, :]).astype(np.float32) / ATT_HEAD_DIM, BF16)
    qg = jnp.tile(q_norm.astype(F32), ATT_HEADS)[None, :]
    kg = jnp.tile(k_norm.astype(F32), ATT_KV_HEADS)[None, :]

    def row_spec(width):
        return pl.BlockSpec((tm, width), lambda i: (i, 0))

    in_specs = [row_spec(D_MODEL), _const_spec((1, D_MODEL))]
    in_specs += [_const_spec((D_MODEL, s)) for s in splits]
    in_specs += [_const_spec((1, ATT_WIDTH)), _const_spec((1, ATT_KV_WIDTH)),
                 pl.BlockSpec((tm, LANES), lambda i: (i % tiles_per_seq, 0)),
                 pl.BlockSpec((tm, LANES), lambda i: (i % tiles_per_seq, 0)),
                 _const_spec((ATT_WIDTH, ATT_WIDTH)), _const_spec((4, HG_WIDTH))]
    out_widths = (ATT_WIDTH, ATT_KV_WIDTH, ATT_KV_WIDTH) + (HG_WIDTH,) * 7 + (N_BRANCH * D_MODEL,)
    out_dtypes = (BF16, BF16, BF16) + (F32,) * 8
    return pl.pallas_call(
        _inproj_kernel,
        grid=(n_tiles,),
        in_specs=in_specs,
        out_specs=[row_spec(w) for w in out_widths],
        out_shape=[jax.ShapeDtypeStruct((t_total, w), dt) for w, dt in zip(out_widths, out_dtypes)],
        compiler_params=_cparams(1),
        name="inproj",
    )(x2, norm_mix.reshape(1, D_MODEL), *ws, qg, kg, cos, sin, m_blk, lb_raw.reshape(4, HG_WIDTH))


def _attention_kernel(q_ref, k_ref, v_ref, o_ref):
    q = q_ref[...]
    k = k_ref[...]
    v = v_ref[...]
    outs = []
    for kvh in range(ATT_KV_HEADS):
        kh = k[:, kvh * ATT_HEAD_DIM:(kvh + 1) * ATT_HEAD_DIM]
        vh = v[:, kvh * ATT_HEAD_DIM:(kvh + 1) * ATT_HEAD_DIM]
        for g in range(ATT_GROUP):
            hd = kvh * ATT_GROUP + g
            qh = q[:, hd * ATT_HEAD_DIM:(hd + 1) * ATT_HEAD_DIM]
            s = _dot_nt(qh, kh)
            m = jnp.max(s, axis=-1, keepdims=True)
            p = jnp.exp(s - m)
            l = jnp.sum(p, axis=-1, keepdims=True)
            outs.append(_dot(p.astype(BF16), vh) / l)
    o_ref[...] = jnp.concatenate(outs, axis=1).astype(BF16)


def _attention(q, k, v, batch, seq_len):
    tq = TQ_ATT
    nq = seq_len // tq
    return pl.pallas_call(
        _attention_kernel,
        grid=(batch, nq),
        in_specs=[pl.BlockSpec((tq, ATT_WIDTH), lambda b, i: (b * nq + i, 0)),
                  pl.BlockSpec((seq_len, ATT_KV_WIDTH), lambda b, i: (b, 0)),
                  pl.BlockSpec((seq_len, ATT_KV_WIDTH), lambda b, i: (b, 0))],
        out_specs=pl.BlockSpec((tq, ATT_WIDTH), lambda b, i: (b * nq + i, 0)),
        out_shape=jax.ShapeDtypeStruct((batch * seq_len, ATT_WIDTH), BF16),
        compiler_params=_cparams(2),
        name="attention",
    )(q, k, v)


def _gla_chunk(q, k, v, lf, st_ref, reverse):
    c = HG_CHUNK
    row = lax.broadcasted_iota(I32, (c, c), 0)
    col = lax.broadcasted_iota(I32, (c, c), 1)
    tri = (col >= row) if reverse else (col <= row)
    tri = tri.astype(BF16)
    hi, mid, lo = _split3(lf)
    b = _dot(tri, hi) + _dot(tri, mid) + _dot(tri, lo)
    edge = 0 if reverse else c - 1
    b_all = b[edge:edge + 1]

    st = st_ref[...]
    o = _dot_nt((q * jnp.exp(b)).astype(BF16), st.astype(BF16))
    k_dec = k * jnp.exp(b_all - b)
    st_ref[...] = st * jnp.exp(b_all) + _dot_tn(v.astype(BF16), k_dec.astype(BF16))

    if reverse:
        blocks = ((0, 32, 32, 64, 32), (0, 16, 16, 32, 16), (32, 48, 48, 64, 48))
    else:
        blocks = ((32, 64, 0, 32, 31), (16, 32, 0, 16, 15), (48, 64, 32, 48, 47))
    pieces = {}
    for r0, r1, c0, c1, ref in blocks:
        b_ref = b[ref:ref + 1]
        qs = q[r0:r1] * jnp.exp(b[r0:r1] - b_ref)
        ks = k[c0:c1] * jnp.exp(b_ref - b[c0:c1])
        a = _dot_nt(qs.astype(BF16), ks.astype(BF16))
        pieces.setdefault(r0, []).append((r1, _dot(a.astype(BF16), v[c0:c1].astype(BF16))))

    out_rows = []
    sub = HG_SUB
    trow = lax.broadcasted_iota(I32, (sub, 1), 0)
    for i in range(c // sub):
        lo_r = i * sub
        qb = q[lo_r:lo_r + sub]
        bb = b[lo_r:lo_r + sub]
        acc = jnp.zeros((sub, HG_DV), F32)
        for s in range(sub):
            keep = (trow <= s) if reverse else (trow >= s)
            dec = jnp.exp(jnp.where(keep, bb - b[lo_r + s:lo_r + s + 1], -jnp.inf))
            a_col = jnp.sum(qb * k[lo_r + s:lo_r + s + 1] * dec, axis=1, keepdims=True)
            acc = acc + a_col * v[lo_r + s:lo_r + s + 1]
        out_rows.append(acc)
    intra = jnp.concatenate(out_rows, axis=0)
    for r0, plist in pieces.items():
        for r1, val in plist:
            pad_top = r0
            pad_bot = c - r1
            parts = []
            if pad_top:
                parts.append(jnp.zeros((pad_top, HG_DV), F32))
            parts.append(val)
            if pad_bot:
                parts.append(jnp.zeros((pad_bot, HG_DV), F32))
            intra = intra + jnp.concatenate(parts, axis=0)
    return o + intra


def _hgrn_kernel(qf_ref, kf_ref, lff_ref, vf_ref, qb_ref, kb_ref, lfb_ref, vb_ref,
                 of_ref, ob_ref, stf_ref, stb_ref):
    @pl.when(pl.program_id(1) == 0)
    def _():
        stf_ref[...] = jnp.zeros_like(stf_ref)
        stb_ref[...] = jnp.zeros_like(stb_ref)

    for refs, o_ref, st_ref, rev in (((qf_ref, kf_ref, vf_ref, lff_ref), of_ref, stf_ref, False),
                                     ((qb_ref, kb_ref, vb_ref, lfb_ref), ob_ref, stb_ref, True)):
        q_r, k_r, v_r, lf_r = refs
        outs = []
        for h in range(HG_HEADS):
            sl = slice(h * HG_DK, (h + 1) * HG_DK)
            outs.append(_gla_chunk(q_r[:, sl], k_r[:, sl], v_r[:, sl], lf_r[:, sl],
                                   st_ref.at[h], rev))
        o_ref[...] = jnp.concatenate(outs, axis=1)


def _hgrn(hq, kf, lff, kb, lfb, hi, batch, seq_len):
    c = HG_CHUNK
    nc = seq_len // c
    fwd = pl.BlockSpec((c, HG_WIDTH), lambda b, i: (b * nc + i, 0))
    bwd = pl.BlockSpec((c, HG_WIDTH), lambda b, i: (b * nc + (nc - 1 - i), 0))
    out = jax.ShapeDtypeStruct((batch * seq_len, HG_WIDTH), F32)
    return pl.pallas_call(
        _hgrn_kernel,
        grid=(batch, nc),
        in_specs=[fwd, fwd, fwd, fwd, bwd, bwd, bwd, bwd],
        out_specs=[fwd, bwd],
        out_shape=[out, out],
        scratch_shapes=[pltpu.VMEM((HG_HEADS, HG_DV, HG_DK), F32),
                        pltpu.VMEM((HG_HEADS, HG_DV, HG_DK), F32)],
        compiler_params=_cparams(2),
        name="hgrn_scan",
    )(hq, kf, lff, hi, hq, kb, lfb, hi)


def _merge_kernel(x_ref, att_ref, of_ref, ob_ref, hg_ref, g_ref, on_ref, wua_ref, wuh_ref,
                  wo_ref, nf_ref, x1_out, hf_out):
    o = of_ref[...] + ob_ref[...]
    gate = hg_ref[...]
    parts = []
    for h in range(HG_HEADS):
        sl = slice(h * HG_DV, (h + 1) * HG_DV)
        parts.append(_rms(o[:, sl], on_ref[...]) * gate[:, sl])
    hn = jnp.concatenate(parts, axis=1).astype(BF16)
    y_hg = _dot(hn, wuh_ref[...])
    y_att = _dot(att_ref[...], wua_ref[...])
    g = g_ref[...]
    merged = g[:, :D_MODEL] * y_att + g[:, D_MODEL:] * y_hg
    x1 = x_ref[...] + _dot(merged.astype(BF16), wo_ref[...])
    x1_out[...] = x1
    hf_out[...] = _rms(x1, nf_ref[...])


def _merge(x2, att, o_f, o_b, hg_s, g, out_norm, w_up_att, w_up_hg, w_out, norm_ffn):
    t_total = x2.shape[0]
    tm = TM_PROJ

    def row_spec(width):
        return pl.BlockSpec((tm, width), lambda i: (i, 0))

    out = jax.ShapeDtypeStruct((t_total, D_MODEL), F32)
    return pl.pallas_call(
        _merge_kernel,
        grid=(t_total // tm,),
        in_specs=[row_spec(D_MODEL), row_spec(ATT_WIDTH), row_spec(HG_WIDTH), row_spec(HG_WIDTH),
                  row_spec(HG_WIDTH), row_spec(N_BRANCH * D_MODEL), _const_spec((1, HG_DV)),
                  _const_spec((ATT_WIDTH, D_MODEL)), _const_spec((HG_WIDTH, D_MODEL)),
                  _const_spec((D_MODEL, D_MODEL)), _const_spec((1, D_MODEL))],
        out_specs=[row_spec(D_MODEL), row_spec(D_MODEL)],
        out_shape=[out, out],
        compiler_params=_cparams(1),
        name="merge",
    )(x2, att, o_f, o_b, hg_s, g, out_norm.reshape(1, HG_DV), w_up_att.astype(BF16),
      w_up_hg.astype(BF16), w_out.astype(BF16), norm_ffn.reshape(1, D_MODEL))


def _topk_rows(s, k, payload=None):
    n = s.shape[0]
    rowid = lax.broadcasted_iota(I32, s.shape, 0)
    vals, picks = [], []
    for _ in range(k):
        m = jnp.max(s, axis=0, keepdims=True)
        pos = jnp.min(jnp.where(s == m, rowid, n), axis=0, keepdims=True)
        hit = rowid == pos
        vals.append(m)
        if payload is None:
            picks.append(pos)
        else:
            picks.append(jnp.max(jnp.where(hit, payload, -1), axis=0, keepdims=True))
        s = jnp.where(hit, -jnp.inf, s)
    return jnp.concatenate(vals, axis=0), jnp.concatenate(picks, axis=0)


def _peer_topk_kernel(hf_ref, wqt_ref, sk_ref, idx_out, g_out):
    qt = _dot_nt(wqt_ref[...], hf_ref[...].astype(BF16)).astype(BF16)
    kk = PEER_TOPK
    half = PEER_DKEY // 2
    for h in range(PEER_HEADS):
        tops = []
        for p in range(2):
            gi = h * 2 + p
            s = _dot(sk_ref[gi], qt[gi * half:(gi + 1) * half])
            tops.append(_topk_rows(s, kk))
        (v1, i1), (v2, i2) = tops
        cand = jnp.concatenate([v1[a:a + 1] + v2 for a in range(kk)], axis=0)
        cand_idx = jnp.concatenate([i1[a:a + 1] * PEER_NKEYS + i2 for a in range(kk)], axis=0)
        top_s, idx = _topk_rows(cand, kk, payload=cand_idx)
        e = jnp.exp(top_s - jnp.max(top_s, axis=0, keepdims=True))
        g_out[h * kk:(h + 1) * kk, :] = e / jnp.sum(e, axis=0, keepdims=True)
        idx_out[h * kk:(h + 1) * kk, :] = idx


def _peer_topk(hf, peer_wq, peer_subkeys):
    t_total = hf.shape[0]
    tm = TM_PEER
    wqt = peer_wq.T.astype(BF16)
    sk = peer_subkeys.reshape(PEER_HEADS * 2, PEER_NKEYS, PEER_DKEY // 2).astype(BF16)
    return pl.pallas_call(
        _peer_topk_kernel,
        grid=(t_total // tm,),
        in_specs=[pl.BlockSpec((tm, D_MODEL), lambda i: (i, 0)),
                  _const_spec((PEER_HEADS * PEER_DKEY, D_MODEL)),
                  _const_spec((PEER_HEADS * 2, PEER_NKEYS, PEER_DKEY // 2))],
        out_specs=[pl.BlockSpec((PEER_PAIRS, tm), lambda i: (0, i)),
                   pl.BlockSpec((PEER_PAIRS, tm), lambda i: (0, i))],
        out_shape=[jax.ShapeDtypeStruct((PEER_PAIRS, t_total), I32),
                   jax.ShapeDtypeStruct((PEER_PAIRS, t_total), F32)],
        compiler_params=_cparams(1),
        name="peer_topk",
    )(hf, wqt, sk)


def _peer_expert_kernel(idx_ref, g_ref, hf_ref, x1_ref, uv_hbm, out_ref, buf, sem):
    n_sub = TM_PEER // PEER_SUB
    rows = PEER_SUB * PEER_PAIRS

    def row_copy(slot, t, j, p):
        return pltpu.make_async_copy(uv_hbm.at[idx_ref[p, t]], buf.at[slot, j * PEER_PAIRS + p],
                                     sem.at[slot])

    def issue(sb, slot):
        for j in range(PEER_SUB):
            t = sb * PEER_SUB + j

            def body(p, carry):
                row_copy(slot, t, j, p).start()
                return carry

            lax.fori_loop(0, PEER_PAIRS, body, 0)

    def wait_all(slot):
        pltpu.make_async_copy(uv_hbm.at[pl.ds(0, rows)], buf.at[slot], sem.at[slot]).wait()

    lane = lax.broadcasted_iota(I32, (PEER_PAIRS, TM_PEER), 1)
    issue(0, 0)

    def sub_block(sb, carry):
        slot = sb % 2

        @pl.when(sb + 1 < n_sub)
        def _():
            issue(sb + 1, 1 - slot)

        wait_all(slot)
        for j in range(PEER_SUB):
            t = sb * PEER_SUB + j
            w = buf[slot, j * PEER_PAIRS:(j + 1) * PEER_PAIRS, :]
            u = pltpu.bitcast(w << 16, F32)
            v = pltpu.bitcast(w & jnp.uint32(0xFFFF0000), F32)
            xrow = hf_ref[pl.ds(t, 1), :]
            prod = u * xrow
            part = prod[:, 0:LANES]
            for c in range(1, D_MODEL // LANES):
                part = part + prod[:, c * LANES:(c + 1) * LANES]
            a = jnp.sum(part, axis=1, keepdims=True)
            gate = jnp.sum(jnp.where(lane == t, g_ref[...], 0.0), axis=1, keepdims=True)
            wgt = jax.nn.gelu(a) * gate
            contrib = jnp.sum(wgt * v, axis=0, keepdims=True)
            out_ref[pl.ds(t, 1), :] = x1_ref[pl.ds(t, 1), :] + contrib
        return carry

    lax.fori_loop(0, n_sub, sub_block, 0)


def _peer_experts(idx_t, g_t, hf, x1, peer_u, peer_v):
    t_total = hf.shape[0]
    tm = TM_PEER
    u16 = lax.bitcast_convert_type(peer_u.astype(BF16), jnp.uint16).astype(U32)
    v16 = lax.bitcast_convert_type(peer_v.astype(BF16), jnp.uint16).astype(U32)
    uv = u16 | (v16 << 16)
    return pl.pallas_call(
        _peer_expert_kernel,
        grid=(t_total // tm,),
        in_specs=[pl.BlockSpec((PEER_PAIRS, tm), lambda i: (0, i), memory_space=pltpu.SMEM),
                  pl.BlockSpec((PEER_PAIRS, tm), lambda i: (0, i)),
                  pl.BlockSpec((tm, D_MODEL), lambda i: (i, 0)),
                  pl.BlockSpec((tm, D_MODEL), lambda i: (i, 0)),
                  pl.BlockSpec(memory_space=pl.ANY)],
        out_specs=pl.BlockSpec((tm, D_MODEL), lambda i: (i, 0)),
        out_shape=jax.ShapeDtypeStruct((t_total, D_MODEL), F32),
        scratch_shapes=[pltpu.VMEM((2, PEER_SUB * PEER_PAIRS, D_MODEL), U32),
                        pltpu.SemaphoreType.DMA((2,))],
        compiler_params=_cparams(1),
        name="peer_experts",
    )(idx_t, g_t, hf, x1, uv)


def _ple_kernel(x_ref, p_ref, np_ref, wg_ref, wp_ref, o_ref):
    x = x_ref[...]
    gate = jax.nn.sigmoid(_dot(_rms(x, np_ref[...]).astype(BF16), wg_ref[...]))
    o_ref[...] = x + gate * _dot(p_ref[...].astype(BF16), wp_ref[...])


def _ple(x2, p2, norm_ple, ple_gate, ple_proj):
    t_total = x2.shape[0]
    tm = TM_PROJ
    return pl.pallas_call(
        _ple_kernel,
        grid=(t_total // tm,),
        in_specs=[pl.BlockSpec((tm, D_MODEL), lambda i: (i, 0)),
                  pl.BlockSpec((tm, PLE_DIM), lambda i: (i, 0)),
                  _const_spec((1, D_MODEL)), _const_spec((D_MODEL, D_MODEL)),
                  _const_spec((PLE_DIM, D_MODEL))],
        out_specs=pl.BlockSpec((tm, D_MODEL), lambda i: (i, 0)),
        out_shape=jax.ShapeDtypeStruct((t_total, D_MODEL), F32),
        compiler_params=_cparams(1),
        name="ple",
    )(x2, p2, norm_ple.reshape(1, D_MODEL), ple_gate.astype(BF16), ple_proj.astype(BF16))


def kernel(x, p, norm_mix, w_in, q_norm, k_norm, hg_lb_raw, hg_out_norm, w_up_att, w_up_hg, w_out,
           norm_ffn, peer_wq, peer_subkeys, peer_u, peer_v, norm_ple, ple_gate, ple_proj):
    batch, seq_len, d = x.shape
    depth = p.shape[0]
    assert d == D_MODEL and depth == 1 and hg_lb_raw.shape == (2, 2, HG_WIDTH)
    assert seq_len % TM_PROJ == 0 and seq_len % HG_CHUNK == 0 and (batch * seq_len) % TM_PEER == 0
    x2 = x.reshape(batch * seq_len, d)
    (q, k, v, hq, kf, lff, kb, lfb, hi, hg_s, g) = _inproj(
        x2, seq_len, norm_mix[0], w_in[0], q_norm[0], k_norm[0], hg_lb_raw)
    att = _attention(q, k, v, batch, seq_len)
    o_f, o_b = _hgrn(hq, kf, lff, kb, lfb, hi, batch, seq_len)
    x1, hf = _merge(x2, att, o_f, o_b, hg_s, g, hg_out_norm[0], w_up_att[0], w_up_hg[0], w_out[0],
                    norm_ffn[0])
    idx_t, g_t = _peer_topk(hf, peer_wq[0], peer_subkeys[0])
    x2b = _peer_experts(idx_t, g_t, hf, x1, peer_u[0], peer_v[0])
    out = _ple(x2b, p[0].reshape(batch * seq_len, PLE_DIM), norm_ple[0], ple_gate[0], ple_proj[0])
    return out.reshape(batch, seq_len, d)
```

```python
import functools
import math

import jax
import jax.numpy as jnp
import numpy as np
from jax import lax
from jax.experimental import pallas as pl
from jax.experimental.pallas import tpu as pltpu

F32 = jnp.float32
BF16 = jnp.bfloat16
I32 = jnp.int32
U32 = jnp.uint32

D_MODEL = 1024
GRID_W = 64
ATT_HEADS = 8
ATT_KV_HEADS = 2
ATT_GROUP = ATT_HEADS // ATT_KV_HEADS
ATT_HEAD_DIM = 64
ATT_WIDTH = ATT_HEADS * ATT_HEAD_DIM
ATT_KV_WIDTH = ATT_KV_HEADS * ATT_HEAD_DIM
ROPE_THETA = 10000.0
HG_HEADS = 4
HG_DK = 128
HG_DV = 128
HG_WIDTH = HG_HEADS * HG_DK
HG_CHUNK = 64
HG_SUB = 16
N_BRANCH = 2
PEER_HEADS = 8
PEER_NKEYS = 128
PEER_DKEY = 256
PEER_TOPK = 16
PEER_PAIRS = PEER_HEADS * PEER_TOPK
PEER_N_EXPERTS = PEER_NKEYS * PEER_NKEYS
PLE_DIM = 256
EPS = 1e-6

LANES = 128
VMEM_LIMIT_BYTES = 56 * 1024 * 1024

TM_PROJ = 256
TQ_ATT = 256
TM_PEER = 128
PEER_SUB = 8


def _cparams(n_axes):
    return pltpu.CompilerParams(
        dimension_semantics=("arbitrary",) * n_axes,
        vmem_limit_bytes=VMEM_LIMIT_BYTES,
    )


def _const_spec(shape):
    nd = len(shape)
    return pl.BlockSpec(shape, lambda *_: (0,) * nd)


def _dot(a, b):
    return jnp.dot(a, b, preferred_element_type=F32)


def _dot_nt(a, b):
    return lax.dot_general(a, b, (((1,), (1,)), ((), ())), preferred_element_type=F32)


def _dot_tn(a, b):
    return lax.dot_general(a, b, (((0,), (0,)), ((), ())), preferred_element_type=F32)


def _split3(x):
    hi = x.astype(BF16)
    r = x - hi.astype(F32)
    mid = r.astype(BF16)
    lo = (r - mid.astype(F32)).astype(BF16)
    return hi, mid, lo


def _rms(x, gain):
    ms = jnp.mean(x * x, axis=-1, keepdims=True)
    return x * lax.rsqrt(ms + EPS) * gain


def _head_rms_rope(a, gain, m_blk, cos, sin_signed, first_half):
    sq = a * a
    hi = sq.astype(BF16)
    lo = (sq - hi.astype(F32)).astype(BF16)
    ms = _dot(hi, m_blk) + _dot(lo, m_blk)
    y = a * lax.rsqrt(ms + EPS) * gain
    w = y.shape[-1]
    nxt = pltpu.roll(y, w - HG_SUB, axis=1)
    prv = pltpu.roll(y, HG_SUB, axis=1)
    partner = jnp.where(first_half, nxt, prv)
    return y * cos + partner * sin_signed


def _inproj_kernel(x_ref, nm_ref, wq_ref, wk_ref, wv_ref, whq_ref, wff_ref, wfb_ref, whi_ref,
                   whg_ref, wg_ref, qg_ref, kg_ref, cos_ref, sin_ref, m_ref, lbraw_ref,
                   q_out, k_out, v_out, hq_out, kf_out, lff_out, kb_out, lfb_out, hi_out,
                   hg_out, g_out):
    h = _rms(x_ref[...], nm_ref[...]).astype(BF16)

    def proj(w_ref):
        return _dot(h, w_ref[...])

    cos = cos_ref[...]
    sin = sin_ref[...]
    lane = lax.broadcasted_iota(I32, (1, LANES), 1)
    first_half = (lane % 32) < 16

    def tile_lanes(t, reps):
        return jnp.concatenate([t] * reps, axis=1) if reps > 1 else t

    rq = ATT_WIDTH // LANES
    q = _head_rms_rope(proj(wq_ref), qg_ref[...], m_ref[...], tile_lanes(cos, rq),
                       tile_lanes(sin, rq), tile_lanes(first_half, rq))
    q_out[...] = (q * (ATT_HEAD_DIM ** -0.5)).astype(BF16)
    k = _head_rms_rope(proj(wk_ref), kg_ref[...], m_ref[0:ATT_KV_WIDTH, 0:ATT_KV_WIDTH],
                       cos, sin, first_half)
    k_out[...] = k.astype(BF16)
    v_out[...] = proj(wv_ref).astype(BF16)

    hq = proj(whq_ref)
    hq_out[...] = hq * jax.nn.sigmoid(hq)

    r = lbraw_ref[...]
    for d, (w_ref, k_ref, lf_ref) in enumerate(((wff_ref, kf_out, lff_out),
                                                 (wfb_ref, kb_out, lfb_out))):
        a0 = r[d:d + 1]
        a1 = r[2 + d:3 + d]
        mx = jnp.maximum(a0, a1)
        e0 = jnp.exp(a0 - mx)
        e1 = jnp.exp(a1 - mx)
        lb = e0 / (e0 + e1)
        f = lb + (1.0 - lb) * jax.nn.sigmoid(proj(w_ref))
        k_ref[...] = 1.0 - f
        lf_ref[...] = jnp.log(f)

    hi_out[...] = proj(whi_ref)
    hg = proj(whg_ref)
    hg_out[...] = hg * jax.nn.sigmoid(hg)
    g_out[...] = jax.nn.sigmoid(proj(wg_ref))


def _rope_tables(seq_len):
    lane = np.arange(LANES)
    axis = (lane % ATT_HEAD_DIM) // 32
    j = lane % 16
    sign = np.where((lane % 32) < 16, -1.0, 1.0).astype(np.float32)
    rot_half = ATT_HEAD_DIM // 4
    inv = ROPE_THETA ** (-jnp.arange(rot_half, dtype=F32) / rot_half)
    t = jnp.arange(seq_len)
    pos = jnp.stack([(t // GRID_W).astype(F32), (t % GRID_W).astype(F32)], axis=1)
    ang = pos[:, axis] * inv[j][None, :]
    return jnp.cos(ang), jnp.sin(ang) * sign[None, :]


def _inproj(x2, seq_len, norm_mix, w_in, q_norm, k_norm, lb_raw):
    t_total = x2.shape[0]
    tm = TM_PROJ
    n_tiles = t_total // tm
    tiles_per_seq = seq_len // tm
    splits = (ATT_WIDTH, ATT_KV_WIDTH, ATT_KV_WIDTH, HG_WIDTH, HG_WIDTH, HG_WIDTH, HG_WIDTH,
              HG_WIDTH, N_BRANCH * D_MODEL)
    offs = np.cumsum((0,) + splits)
    w_bf = w_in.astype(BF16)
    ws = [w_bf[:, offs[i]:offs[i + 1]] for i in range(len(splits))]
    cos, sin = _rope_tables(seq_len)
    grp = np.arange(ATT_WIDTH) // ATT_HEAD_DIM
    m_blk = jnp.asarray((grp[:, None] == grp[None, :]).astype(np.float32) / ATT_HEAD_DIM, BF16)
    qg = jnp.tile(q_norm.astype(F32), ATT_HEADS)[None, :]
    kg = jnp.tile(k_norm.astype(F32), ATT_KV_HEADS)[None, :]

    def row_spec(width):
        return pl.BlockSpec((tm, width), lambda i: (i, 0))

    in_specs = [row_spec(D_MODEL), _const_spec((1, D_MODEL))]
    in_specs += [_const_spec((D_MODEL, s)) for s in splits]
    in_specs += [_const_spec((1, ATT_WIDTH)), _const_spec((1, ATT_KV_WIDTH)),
                 pl.BlockSpec((tm, LANES), lambda i: (i % tiles_per_seq, 0)),
                 pl.BlockSpec((tm, LANES), lambda i: (i % tiles_per_seq, 0)),
                 _const_spec((ATT_WIDTH, ATT_WIDTH)), _const_spec((4, HG_WIDTH))]
    out_widths = (ATT_WIDTH, ATT_KV_WIDTH, ATT_KV_WIDTH) + (HG_WIDTH,) * 7 + (N_BRANCH * D_MODEL,)
    out_dtypes = (BF16, BF16, BF16) + (F32,) * 8
    return pl.pallas_call(
        _inproj_kernel,
        grid=(n_tiles,),
        in_specs=in_specs,
        out_specs=[row_spec(w) for w in out_widths],
        out_shape=[jax.ShapeDtypeStruct((t_total, w), dt) for w, dt in zip(out_widths, out_dtypes)],
        compiler_params=_cparams(1),
        name="inproj",
    )(x2, norm_mix.reshape(1, D_MODEL), *ws, qg, kg, cos, sin, m_blk, lb_raw.reshape(4, HG_WIDTH))


def _attention_kernel(q_ref, k_ref, v_ref, o_ref):
    q = q_ref[...]
    k = k_ref[...]
    v = v_ref[...]
    outs = []
    for kvh in range(ATT_KV_HEADS):
        kh = k[:, kvh * ATT_HEAD_DIM:(kvh + 1) * ATT_HEAD_DIM]
        vh = v[:, kvh * ATT_HEAD_DIM:(kvh + 1) * ATT_HEAD_DIM]
        for g in range(ATT_GROUP):
            hd = kvh * ATT_GROUP + g
            qh = q[:, hd * ATT_HEAD_DIM:(hd + 1) * ATT_HEAD_DIM]
            s = _dot_nt(qh, kh)
            m = jnp.max(s, axis=-1, keepdims=True)
            p = jnp.exp(s - m)
            l = jnp.sum(p, axis=-1, keepdims=True)
            outs.append(_dot(p.astype(BF16), vh) / l)
    o_ref[...] = jnp.concatenate(outs, axis=1).astype(BF16)


def _attention(q, k, v, batch, seq_len):
    tq = TQ_ATT
    nq = seq_len // tq
    return pl.pallas_call(
        _attention_kernel,
        grid=(batch, nq),
        in_specs=[pl.BlockSpec((tq, ATT_WIDTH), lambda b, i: (b * nq + i, 0)),
                  pl.BlockSpec((seq_len, ATT_KV_WIDTH), lambda b, i: (b, 0)),
                  pl.BlockSpec((seq_len, ATT_KV_WIDTH), lambda b, i: (b, 0))],
        out_specs=pl.BlockSpec((tq, ATT_WIDTH), lambda b, i: (b * nq + i, 0)),
        out_shape=jax.ShapeDtypeStruct((batch * seq_len, ATT_WIDTH), BF16),
        compiler_params=_cparams(2),
        name="attention",
    )(q, k, v)


def _gla_chunk(q, k, v, lf, st_ref, reverse):
    c = HG_CHUNK
    row = lax.broadcasted_iota(I32, (c, c), 0)
    col = lax.broadcasted_iota(I32, (c, c), 1)
    tri = (col >= row) if reverse else (col <= row)
    tri = tri.astype(BF16)
    hi, mid, lo = _split3(lf)
    b = _dot(tri, hi) + _dot(tri, mid) + _dot(tri, lo)
    edge = 0 if reverse else c - 1
    b_all = b[edge:edge + 1]

    st = st_ref[...]
    o = _dot_nt((q * jnp.exp(b)).astype(BF16), st.astype(BF16))
    k_dec = k * jnp.exp(b_all - b)
    st_ref[...] = st * jnp.exp(b_all) + _dot_tn(v.astype(BF16), k_dec.astype(BF16))

    if reverse:
        blocks = ((0, 32, 32, 64, 32), (0, 16, 16, 32, 16), (32, 48, 48, 64, 48))
    else:
        blocks = ((32, 64, 0, 32, 31), (16, 32, 0, 16, 15), (48, 64, 32, 48, 47))
    pieces = {}
    for r0, r1, c0, c1, ref in blocks:
        b_ref = b[ref:ref + 1]
        qs = q[r0:r1] * jnp.exp(b[r0:r1] - b_ref)
        ks = k[c0:c1] * jnp.exp(b_ref - b[c0:c1])
        a = _dot_nt(qs.astype(BF16), ks.astype(BF16))
        pieces.setdefault(r0, []).append((r1, _dot(a.astype(BF16), v[c0:c1].astype(BF16))))

    out_rows = []
    sub = HG_SUB
    trow = lax.broadcasted_iota(I32, (sub, 1), 0)
    for i in range(c // sub):
        lo_r = i * sub
        qb = q[lo_r:lo_r + sub]
        bb = b[lo_r:lo_r + sub]
        acc = jnp.zeros((sub, HG_DV), F32)
        for s in range(sub):
            keep = (trow <= s) if reverse else (trow >= s)
            dec = jnp.exp(jnp.where(keep, bb - b[lo_r + s:lo_r + s + 1], -jnp.inf))
            a_col = jnp.sum(qb * k[lo_r + s:lo_r + s + 1] * dec, axis=1, keepdims=True)
            acc = acc + a_col * v[lo_r + s:lo_r + s + 1]
        out_rows.append(acc)
    intra = jnp.concatenate(out_rows, axis=0)
    for r0, plist in pieces.items():
        for r1, val in plist:
            pad_top = r0
            pad_bot = c - r1
            parts = []
            if pad_top:
                parts.append(jnp.zeros((pad_top, HG_DV), F32))
            parts.append(val)
            if pad_bot:
                parts.append(jnp.zeros((pad_bot, HG_DV), F32))
            intra = intra + jnp.concatenate(parts, axis=0)
    return o + intra


def _hgrn_kernel(qf_ref, kf_ref, lff_ref, vf_ref, qb_ref, kb_ref, lfb_ref, vb_ref,
                 of_ref, ob_ref, stf_ref, stb_ref):
    @pl.when(pl.program_id(1) == 0)
    def _():
        stf_ref[...] = jnp.zeros_like(stf_ref)
        stb_ref[...] = jnp.zeros_like(stb_ref)

    for refs, o_ref, st_ref, rev in (((qf_ref, kf_ref, vf_ref, lff_ref), of_ref, stf_ref, False),
                                     ((qb_ref, kb_ref, vb_ref, lfb_ref), ob_ref, stb_ref, True)):
        q_r, k_r, v_r, lf_r = refs
        outs = []
        for h in range(HG_HEADS):
            sl = slice(h * HG_DK, (h + 1) * HG_DK)
            outs.append(_gla_chunk(q_r[:, sl], k_r[:, sl], v_r[:, sl], lf_r[:, sl],
                                   st_ref.at[h], rev))
        o_ref[...] = jnp.concatenate(outs, axis=1)


def _hgrn(hq, kf, lff, kb, lfb, hi, batch, seq_len):
    c = HG_CHUNK
    nc = seq_len // c
    fwd = pl.BlockSpec((c, HG_WIDTH), lambda b, i: (b * nc + i, 0))
    bwd = pl.BlockSpec((c, HG_WIDTH), lambda b, i: (b * nc + (nc - 1 - i), 0))
    out = jax.ShapeDtypeStruct((batch * seq_len, HG_WIDTH), F32)
    return pl.pallas_call(
        _hgrn_kernel,
        grid=(batch, nc),
        in_specs=[fwd, fwd, fwd, fwd, bwd, bwd, bwd, bwd],
        out_specs=[fwd, bwd],
        out_shape=[out, out],
        scratch_shapes=[pltpu.VMEM((HG_HEADS, HG_DV, HG_DK), F32),
                        pltpu.VMEM((HG_HEADS, HG_DV, HG_DK), F32)],
        compiler_params=_cparams(2),
        name="hgrn_scan",
    )(hq, kf, lff, hi, hq, kb, lfb, hi)


def _merge_kernel(x_ref, att_ref, of_ref, ob_ref, hg_ref, g_ref, on_ref, wua_ref, wuh_ref,
                  wo_ref, nf_ref, x1_out, hf_out):
    o = of_ref[...] + ob_ref[...]
    gate = hg_ref[...]
    parts = []
    for h in range(HG_HEADS):
        sl = slice(h * HG_DV, (h + 1) * HG_DV)
        parts.append(_rms(o[:, sl], on_ref[...]) * gate[:, sl])
    hn = jnp.concatenate(parts, axis=1).astype(BF16)
    y_hg = _dot(hn, wuh_ref[...])
    y_att = _dot(att_ref[...], wua_ref[...])
    g = g_ref[...]
    merged = g[:, :D_MODEL] * y_att + g[:, D_MODEL:] * y_hg
    x1 = x_ref[...] + _dot(merged.astype(BF16), wo_ref[...])
    x1_out[...] = x1
    hf_out[...] = _rms(x1, nf_ref[...])


def _merge(x2, att, o_f, o_b, hg_s, g, out_norm, w_up_att, w_up_hg, w_out, norm_ffn):
    t_total = x2.shape[0]
    tm = TM_PROJ

    def row_spec(width):
        return pl.BlockSpec((tm, width), lambda i: (i, 0))

    out = jax.ShapeDtypeStruct((t_total, D_MODEL), F32)
    return pl.pallas_call(
        _merge_kernel,
        grid=(t_total // tm,),
        in_specs=[row_spec(D_MODEL), row_spec(ATT_WIDTH), row_spec(HG_WIDTH), row_spec(HG_WIDTH),
                  row_spec(HG_WIDTH), row_spec(N_BRANCH * D_MODEL), _const_spec((1, HG_DV)),
                  _const_spec((ATT_WIDTH, D_MODEL)), _const_spec((HG_WIDTH, D_MODEL)),
                  _const_spec((D_MODEL, D_MODEL)), _const_spec((1, D_MODEL))],
        out_specs=[row_spec(D_MODEL), row_spec(D_MODEL)],
        out_shape=[out, out],
        compiler_params=_cparams(1),
        name="merge",
    )(x2, att, o_f, o_b, hg_s, g, out_norm.reshape(1, HG_DV), w_up_att.astype(BF16),
      w_up_hg.astype(BF16), w_out.astype(BF16), norm_ffn.reshape(1, D_MODEL))


def _topk_rows(s, k, key):
    big = jnp.float32(2.0 ** 30)
    vals, picks = [], []
    for _ in range(k):
        m = jnp.max(s, axis=0, keepdims=True)
        kmin = jnp.min(jnp.where(s == m, key, big), axis=0, keepdims=True)
        vals.append(m)
        picks.append(kmin)
        s = jnp.where(key == kmin, -jnp.inf, s)
    return jnp.concatenate(vals, axis=0), jnp.concatenate(picks, axis=0)


def _peer_topk_kernel(hf_ref, wqt_ref, sk_ref, idx_out, g_out):
    qt = _dot_nt(wqt_ref[...], hf_ref[...].astype(BF16)).astype(BF16)
    kk = PEER_TOPK
    half = PEER_DKEY // 2
    tokens = hf_ref.shape[0]
    key_row = lax.broadcasted_iota(I32, (PEER_NKEYS, tokens), 0).astype(F32)
    pos_scaled = (lax.broadcasted_iota(I32, (kk * kk, tokens), 0) * PEER_N_EXPERTS).astype(F32)
    for h in range(PEER_HEADS):
        tops = []
        for p in range(2):
            gi = h * 2 + p
            s = _dot(sk_ref[gi], qt[gi * half:(gi + 1) * half])
            tops.append(_topk_rows(s, kk, key_row))
        (v1, i1), (v2, i2) = tops
        cand = jnp.concatenate([v1[a:a + 1] + v2 for a in range(kk)], axis=0)
        cand_idx = jnp.concatenate([i1[a:a + 1] * PEER_NKEYS + i2 for a in range(kk)], axis=0)
        top_s, key = _topk_rows(cand, kk, pos_scaled + cand_idx)
        idx = key - jnp.floor(key * (1.0 / PEER_N_EXPERTS)) * PEER_N_EXPERTS
        e = jnp.exp(top_s - jnp.max(top_s, axis=0, keepdims=True))
        g_out[h * kk:(h + 1) * kk, :] = e / jnp.sum(e, axis=0, keepdims=True)
        idx_out[h * kk:(h + 1) * kk, :] = idx.astype(I32)


def _peer_topk(hf, peer_wq, peer_subkeys):
    t_total = hf.shape[0]
    tm = TM_PEER
    wqt = peer_wq.T.astype(BF16)
    sk = peer_subkeys.reshape(PEER_HEADS * 2, PEER_NKEYS, PEER_DKEY // 2).astype(BF16)
    return pl.pallas_call(
        _peer_topk_kernel,
        grid=(t_total // tm,),
        in_specs=[pl.BlockSpec((tm, D_MODEL), lambda i: (i, 0)),
                  _const_spec((PEER_HEADS * PEER_DKEY, D_MODEL)),
                  _const_spec((PEER_HEADS * 2, PEER_NKEYS, PEER_DKEY // 2))],
        out_specs=[pl.BlockSpec((PEER_PAIRS, tm), lambda i: (0, i)),
                   pl.BlockSpec((PEER_PAIRS, tm), lambda i: (0, i))],
        out_shape=[jax.ShapeDtypeStruct((PEER_PAIRS, t_total), I32),
                   jax.ShapeDtypeStruct((PEER_PAIRS, t_total), F32)],
        compiler_params=_cparams(1),
        name="peer_topk",
    )(hf, wqt, sk)


def _peer_expert_kernel(idx_ref, idxn_ref, g_ref, hf_ref, x1_ref, p_ref, np_ref, wg_ref, wp_ref,
                        uv_hbm, out_ref, x2_scr, buf, sem):
    i = pl.program_id(0)
    last = pl.num_programs(0) - 1
    n_sub = TM_PEER // PEER_SUB
    rows = PEER_SUB * PEER_PAIRS
    lane = lax.broadcasted_iota(I32, (PEER_PAIRS, TM_PEER), 1)

    def issue_token(src_idx, t, slot, j):
        for p in range(PEER_PAIRS):
            pltpu.make_async_copy(uv_hbm.at[src_idx[p, t]], buf.at[slot, j * PEER_PAIRS + p],
                                  sem.at[slot]).start(priority=p % 2)

    def wait_all(slot):
        pltpu.make_async_copy(uv_hbm.at[pl.ds(0, rows)], buf.at[slot], sem.at[slot]).wait()

    def token(t, slot, j):
        w = buf[slot, j * PEER_PAIRS:(j + 1) * PEER_PAIRS, :]
        u = lax.bitcast_convert_type(w << 16, F32)
        v = lax.bitcast_convert_type(w & jnp.uint32(0xFFFF0000), F32)
        prod = u * hf_ref[pl.ds(t, 1), :]
        part = prod[:, 0:LANES]
        for c in range(1, D_MODEL // LANES):
            part = part + prod[:, c * LANES:(c + 1) * LANES]
        a = jnp.sum(part, axis=1, keepdims=True)
        gate = jnp.sum(jnp.where(lane == t, g_ref[...], 0.0), axis=1, keepdims=True)
        wgt = jax.nn.gelu(a) * gate
        contrib = jnp.sum(wgt * v, axis=0, keepdims=True)
        x2_scr[pl.ds(t, 1), :] = x1_ref[pl.ds(t, 1), :] + contrib

    @pl.when(i == 0)
    def _():
        for j in range(PEER_SUB):
            issue_token(idx_ref, j, 0, j)

    def sub_block(sb, carry):
        slot = sb % 2
        wait_all(slot)
        for j in range(PEER_SUB):
            t = sb * PEER_SUB + j
            issue_token(idx_ref, t + PEER_SUB, 1 - slot, j)
            token(t, slot, j)
        return carry

    lax.fori_loop(0, n_sub - 1, sub_block, 0)

    wait_all(1)
    for j in range(PEER_SUB):
        issue_token(idxn_ref, j, 0, j)
        token((n_sub - 1) * PEER_SUB + j, 1, j)

    @pl.when(i == last)
    def _():
        wait_all(0)

    x2 = x2_scr[...]
    gate = jax.nn.sigmoid(_dot(_rms(x2, np_ref[...]).astype(BF16), wg_ref[...]))
    out_ref[...] = x2 + gate * _dot(p_ref[...].astype(BF16), wp_ref[...])


def _peer_experts_ple(idx_t, g_t, hf, x1, peer_u, peer_v, p2, norm_ple, ple_gate, ple_proj):
    t_total = hf.shape[0]
    tm = TM_PEER
    n = t_total // tm
    assert (tm // PEER_SUB) % 2 == 0
    u16 = lax.bitcast_convert_type(peer_u.astype(BF16), jnp.uint16).astype(U32)
    v16 = lax.bitcast_convert_type(peer_v.astype(BF16), jnp.uint16).astype(U32)
    uv = u16 | (v16 << 16)
    return pl.pallas_call(
        _peer_expert_kernel,
        grid=(n,),
        in_specs=[pl.BlockSpec((PEER_PAIRS, tm), lambda i: (0, i), memory_space=pltpu.SMEM),
                  pl.BlockSpec((PEER_PAIRS, tm), lambda i: (0, jnp.minimum(i + 1, n - 1)),
                               memory_space=pltpu.SMEM),
                  pl.BlockSpec((PEER_PAIRS, tm), lambda i: (0, i)),
                  pl.BlockSpec((tm, D_MODEL), lambda i: (i, 0)),
                  pl.BlockSpec((tm, D_MODEL), lambda i: (i, 0)),
                  pl.BlockSpec((tm, PLE_DIM), lambda i: (i, 0)),
                  _const_spec((1, D_MODEL)), _const_spec((D_MODEL, D_MODEL)),
                  _const_spec((PLE_DIM, D_MODEL)),
                  pl.BlockSpec(memory_space=pl.ANY)],
        out_specs=pl.BlockSpec((tm, D_MODEL), lambda i: (i, 0)),
        out_shape=jax.ShapeDtypeStruct((t_total, D_MODEL), F32),
        scratch_shapes=[pltpu.VMEM((tm, D_MODEL), F32),
                        pltpu.VMEM((2, PEER_SUB * PEER_PAIRS, D_MODEL), U32),
                        pltpu.SemaphoreType.DMA((2,))],
        compiler_params=pltpu.CompilerParams(
            dimension_semantics=("arbitrary",), vmem_limit_bytes=VMEM_LIMIT_BYTES,
            disable_bounds_checks=True),
        name="peer_experts",
    )(idx_t, idx_t, g_t, hf, x1, p2, norm_ple.reshape(1, D_MODEL), ple_gate.astype(BF16),
      ple_proj.astype(BF16), uv)


def kernel(x, p, norm_mix, w_in, q_norm, k_norm, hg_lb_raw, hg_out_norm, w_up_att, w_up_hg, w_out,
           norm_ffn, peer_wq, peer_subkeys, peer_u, peer_v, norm_ple, ple_gate, ple_proj):
    batch, seq_len, d = x.shape
    depth = p.shape[0]
    assert d == D_MODEL and depth == 1 and hg_lb_raw.shape == (2, 2, HG_WIDTH)
    assert seq_len % TM_PROJ == 0 and seq_len % HG_CHUNK == 0 and (batch * seq_len) % TM_PEER == 0
    x2 = x.reshape(batch * seq_len, d)
    (q, k, v, hq, kf, lff, kb, lfb, hi, hg_s, g) = _inproj(
        x2, seq_len, norm_mix[0], w_in[0], q_norm[0], k_norm[0], hg_lb_raw)
    att = _attention(q, k, v, batch, seq_len)
    o_f, o_b = _hgrn(hq, kf, lff, kb, lfb, hi, batch, seq_len)
    x1, hf = _merge(x2, att, o_f, o_b, hg_s, g, hg_out_norm[0], w_up_att[0], w_up_hg[0], w_out[0],
                    norm_ffn[0])
    idx_t, g_t = _peer_topk(hf, peer_wq[0], peer_subkeys[0])
    out = _peer_experts_ple(idx_t, g_t, hf, x1, peer_u[0], peer_v[0],
                            p[0].reshape(batch * seq_len, PLE_DIM), norm_ple[0], ple_gate[0],
                            ple_proj[0])
    return out.reshape(batch, seq_len, d)
```

```python
import functools
import math

import jax
import jax.numpy as jnp
import numpy as np
from jax import lax
from jax.experimental import pallas as pl
from jax.experimental.pallas import tpu as pltpu

F32 = jnp.float32
BF16 = jnp.bfloat16
I32 = jnp.int32
U32 = jnp.uint32

D_MODEL = 1024
GRID_W = 64
ATT_HEADS = 8
ATT_KV_HEADS = 2
ATT_GROUP = ATT_HEADS // ATT_KV_HEADS
ATT_HEAD_DIM = 64
ATT_WIDTH = ATT_HEADS * ATT_HEAD_DIM
ATT_KV_WIDTH = ATT_KV_HEADS * ATT_HEAD_DIM
ROPE_THETA = 10000.0
HG_HEADS = 4
HG_DK = 128
HG_DV = 128
HG_WIDTH = HG_HEADS * HG_DK
HG_CHUNK = 64
HG_SUB = 16
N_BRANCH = 2
PEER_HEADS = 8
PEER_NKEYS = 128
PEER_DKEY = 256
PEER_TOPK = 16
PEER_PAIRS = PEER_HEADS * PEER_TOPK
PEER_N_EXPERTS = PEER_NKEYS * PEER_NKEYS
PLE_DIM = 256
EPS = 1e-6

LANES = 128
VMEM_LIMIT_BYTES = 56 * 1024 * 1024

TM_PROJ = 256
TQ_ATT = 256
TM_PEER = 128
PEER_SUB = 8


def _cparams(n_axes):
    return pltpu.CompilerParams(
        dimension_semantics=("arbitrary",) * n_axes,
        vmem_limit_bytes=VMEM_LIMIT_BYTES,
    )


def _const_spec(shape):
    nd = len(shape)
    return pl.BlockSpec(shape, lambda *_: (0,) * nd)


def _dot(a, b):
    return jnp.dot(a, b, preferred_element_type=F32)


def _dot_nt(a, b):
    return lax.dot_general(a, b, (((1,), (1,)), ((), ())), preferred_element_type=F32)


def _dot_tn(a, b):
    return lax.dot_general(a, b, (((0,), (0,)), ((), ())), preferred_element_type=F32)


def _split3(x):
    hi = x.astype(BF16)
    r = x - hi.astype(F32)
    mid = r.astype(BF16)
    lo = (r - mid.astype(F32)).astype(BF16)
    return hi, mid, lo


def _rms(x, gain):
    ms = jnp.mean(x * x, axis=-1, keepdims=True)
    return x * lax.rsqrt(ms + EPS) * gain


def _head_rms_rope(a, gain, m_blk, cos, sin_signed, first_half):
    sq = a * a
    hi = sq.astype(BF16)
    lo = (sq - hi.astype(F32)).astype(BF16)
    ms = _dot(hi, m_blk) + _dot(lo, m_blk)
    y = a * lax.rsqrt(ms + EPS) * gain
    w = y.shape[-1]
    nxt = pltpu.roll(y, w - HG_SUB, axis=1)
    prv = pltpu.roll(y, HG_SUB, axis=1)
    partner = jnp.where(first_half, nxt, prv)
    return y * cos + partner * sin_signed


def _inproj_kernel(x_ref, nm_ref, wq_ref, wk_ref, wv_ref, whq_ref, wff_ref, wfb_ref, whi_ref,
                   whg_ref, wg_ref, qg_ref, kg_ref, cos_ref, sin_ref, m_ref, lbraw_ref,
                   q_out, k_out, v_out, hq_out, kf_out, lff_out, kb_out, lfb_out, hi_out,
                   hg_out, g_out):
    h = _rms(x_ref[...], nm_ref[...]).astype(BF16)

    def proj(w_ref):
        return _dot(h, w_ref[...])

    cos = cos_ref[...]
    sin = sin_ref[...]
    lane = lax.broadcasted_iota(I32, (1, LANES), 1)
    first_half = (lane % 32) < 16

    def tile_lanes(t, reps):
        return jnp.concatenate([t] * reps, axis=1) if reps > 1 else t

    rq = ATT_WIDTH // LANES
    q = _head_rms_rope(proj(wq_ref), qg_ref[...], m_ref[...], tile_lanes(cos, rq),
                       tile_lanes(sin, rq), tile_lanes(first_half, rq))
    q_out[...] = (q * (ATT_HEAD_DIM ** -0.5)).astype(BF16)
    k = _head_rms_rope(proj(wk_ref), kg_ref[...], m_ref[0:ATT_KV_WIDTH, 0:ATT_KV_WIDTH],
                       cos, sin, first_half)
    k_out[...] = k.astype(BF16)
    v_out[...] = proj(wv_ref).astype(BF16)

    hq = proj(whq_ref)
    hq_out[...] = hq * jax.nn.sigmoid(hq)

    r = lbraw_ref[...]
    for d, (w_ref, k_ref, lf_ref) in enumerate(((wff_ref, kf_out, lff_out),
                                                 (wfb_ref, kb_out, lfb_out))):
        a0 = r[d:d + 1]
        a1 = r[2 + d:3 + d]
        mx = jnp.maximum(a0, a1)
        e0 = jnp.exp(a0 - mx)
        e1 = jnp.exp(a1 - mx)
        lb = e0 / (e0 + e1)
        f = lb + (1.0 - lb) * jax.nn.sigmoid(proj(w_ref))
        k_ref[...] = 1.0 - f
        lf_ref[...] = jnp.log(f)

    hi_out[...] = proj(whi_ref)
    hg = proj(whg_ref)
    hg_out[...] = hg * jax.nn.sigmoid(hg)
    g_out[...] = jax.nn.sigmoid(proj(wg_ref))


def _rope_tables(seq_len):
    lane = np.arange(LANES)
    axis = (lane % ATT_HEAD_DIM) // 32
    j = lane % 16
    sign = np.where((lane % 32) < 16, -1.0, 1.0).astype(np.float32)
    rot_half = ATT_HEAD_DIM // 4
    inv = ROPE_THETA ** (-jnp.arange(rot_half, dtype=F32) / rot_half)
    t = jnp.arange(seq_len)
    pos = jnp.stack([(t // GRID_W).astype(F32), (t % GRID_W).astype(F32)], axis=1)
    ang = pos[:, axis] * inv[j][None, :]
    return jnp.cos(ang), jnp.sin(ang) * sign[None, :]


def _inproj(x2, seq_len, norm_mix, w_in, q_norm, k_norm, lb_raw):
    t_total = x2.shape[0]
    tm = TM_PROJ
    n_tiles = t_total // tm
    tiles_per_seq = seq_len // tm
    splits = (ATT_WIDTH, ATT_KV_WIDTH, ATT_KV_WIDTH, HG_WIDTH, HG_WIDTH, HG_WIDTH, HG_WIDTH,
              HG_WIDTH, N_BRANCH * D_MODEL)
    offs = np.cumsum((0,) + splits)
    w_bf = w_in.astype(BF16)
    ws = [w_bf[:, offs[i]:offs[i + 1]] for i in range(len(splits))]
    cos, sin = _rope_tables(seq_len)
    grp = np.arange(ATT_WIDTH) // ATT_HEAD_DIM
    m_blk = jnp.asarray((grp[:, None] == grp[None, :]).astype(np.float32) / ATT_HEAD_DIM, BF16)
    qg = jnp.tile(q_norm.astype(F32), ATT_HEADS)[None, :]
    kg = jnp.tile(k_norm.astype(F32), ATT_KV_HEADS)[None, :]

    def row_spec(width):
        return pl.BlockSpec((tm, width), lambda i: (i, 0))

    in_specs = [row_spec(D_MODEL), _const_spec((1, D_MODEL))]
    in_specs += [_const_spec((D_MODEL, s)) for s in splits]
    in_specs += [_const_spec((1, ATT_WIDTH)), _const_spec((1, ATT_KV_WIDTH)),
                 pl.BlockSpec((tm, LANES), lambda i: (i % tiles_per_seq, 0)),
                 pl.BlockSpec((tm, LANES), lambda i: (i % tiles_per_seq, 0)),
                 _const_spec((ATT_WIDTH, ATT_WIDTH)), _const_spec((4, HG_WIDTH))]
    out_widths = (ATT_WIDTH, ATT_KV_WIDTH, ATT_KV_WIDTH) + (HG_WIDTH,) * 7 + (N_BRANCH * D_MODEL,)
    out_dtypes = (BF16, BF16, BF16) + (F32,) * 8
    return pl.pallas_call(
        _inproj_kernel,
        grid=(n_tiles,),
        in_specs=in_specs,
        out_specs=[row_spec(w) for w in out_widths],
        out_shape=[jax.ShapeDtypeStruct((t_total, w), dt) for w, dt in zip(out_widths, out_dtypes)],
        compiler_params=_cparams(1),
        name="inproj",
    )(x2, norm_mix.reshape(1, D_MODEL), *ws, qg, kg, cos, sin, m_blk, lb_raw.reshape(4, HG_WIDTH))


def _attention_kernel(q_ref, k_ref, v_ref, o_ref):
    q = q_ref[...]
    k = k_ref[...]
    v = v_ref[...]
    outs = []
    for kvh in range(ATT_KV_HEADS):
        kh = k[:, kvh * ATT_HEAD_DIM:(kvh + 1) * ATT_HEAD_DIM]
        vh = v[:, kvh * ATT_HEAD_DIM:(kvh + 1) * ATT_HEAD_DIM]
        for g in range(ATT_GROUP):
            hd = kvh * ATT_GROUP + g
            qh = q[:, hd * ATT_HEAD_DIM:(hd + 1) * ATT_HEAD_DIM]
            s = _dot_nt(qh, kh)
            m = jnp.max(s, axis=-1, keepdims=True)
            p = jnp.exp(s - m)
            l = jnp.sum(p, axis=-1, keepdims=True)
            outs.append(_dot(p.astype(BF16), vh) / l)
    o_ref[...] = jnp.concatenate(outs, axis=1).astype(BF16)


def _attention(q, k, v, batch, seq_len):
    tq = TQ_ATT
    nq = seq_len // tq
    return pl.pallas_call(
        _attention_kernel,
        grid=(batch, nq),
        in_specs=[pl.BlockSpec((tq, ATT_WIDTH), lambda b, i: (b * nq + i, 0)),
                  pl.BlockSpec((seq_len, ATT_KV_WIDTH), lambda b, i: (b, 0)),
                  pl.BlockSpec((seq_len, ATT_KV_WIDTH), lambda b, i: (b, 0))],
        out_specs=pl.BlockSpec((tq, ATT_WIDTH), lambda b, i: (b * nq + i, 0)),
        out_shape=jax.ShapeDtypeStruct((batch * seq_len, ATT_WIDTH), BF16),
        compiler_params=_cparams(2),
        name="attention",
    )(q, k, v)


def _gla_chunk(q, k, v, lf, st_ref, reverse):
    c = HG_CHUNK
    row = lax.broadcasted_iota(I32, (c, c), 0)
    col = lax.broadcasted_iota(I32, (c, c), 1)
    tri = (col >= row) if reverse else (col <= row)
    tri = tri.astype(BF16)
    hi, mid, lo = _split3(lf)
    b = _dot(tri, hi) + _dot(tri, mid) + _dot(tri, lo)
    edge = 0 if reverse else c - 1
    b_all = b[edge:edge + 1]

    st = st_ref[...]
    o = _dot_nt((q * jnp.exp(b)).astype(BF16), st.astype(BF16))
    k_dec = k * jnp.exp(b_all - b)
    st_ref[...] = st * jnp.exp(b_all) + _dot_tn(v.astype(BF16), k_dec.astype(BF16))

    if reverse:
        blocks = ((0, 32, 32, 64, 32), (0, 16, 16, 32, 16), (32, 48, 48, 64, 48))
    else:
        blocks = ((32, 64, 0, 32, 31), (16, 32, 0, 16, 15), (48, 64, 32, 48, 47))
    pieces = {}
    for r0, r1, c0, c1, ref in blocks:
        b_ref = b[ref:ref + 1]
        qs = q[r0:r1] * jnp.exp(b[r0:r1] - b_ref)
        ks = k[c0:c1] * jnp.exp(b_ref - b[c0:c1])
        a = _dot_nt(qs.astype(BF16), ks.astype(BF16))
        pieces.setdefault(r0, []).append((r1, _dot(a.astype(BF16), v[c0:c1].astype(BF16))))

    out_rows = []
    sub = HG_SUB
    trow = lax.broadcasted_iota(I32, (sub, 1), 0)
    for i in range(c // sub):
        lo_r = i * sub
        qb = q[lo_r:lo_r + sub]
        bb = b[lo_r:lo_r + sub]
        acc = jnp.zeros((sub, HG_DV), F32)
        for s in range(sub):
            keep = (trow <= s) if reverse else (trow >= s)
            dec = jnp.exp(jnp.where(keep, bb - b[lo_r + s:lo_r + s + 1], -jnp.inf))
            a_col = jnp.sum(qb * k[lo_r + s:lo_r + s + 1] * dec, axis=1, keepdims=True)
            acc = acc + a_col * v[lo_r + s:lo_r + s + 1]
        out_rows.append(acc)
    intra = jnp.concatenate(out_rows, axis=0)
    for r0, plist in pieces.items():
        for r1, val in plist:
            pad_top = r0
            pad_bot = c - r1
            parts = []
            if pad_top:
                parts.append(jnp.zeros((pad_top, HG_DV), F32))
            parts.append(val)
            if pad_bot:
                parts.append(jnp.zeros((pad_bot, HG_DV), F32))
            intra = intra + jnp.concatenate(parts, axis=0)
    return o + intra


def _hgrn_kernel(qf_ref, kf_ref, lff_ref, vf_ref, qb_ref, kb_ref, lfb_ref, vb_ref,
                 of_ref, ob_ref, stf_ref, stb_ref):
    @pl.when(pl.program_id(1) == 0)
    def _():
        stf_ref[...] = jnp.zeros_like(stf_ref)
        stb_ref[...] = jnp.zeros_like(stb_ref)

    for refs, o_ref, st_ref, rev in (((qf_ref, kf_ref, vf_ref, lff_ref), of_ref, stf_ref, False),
                                     ((qb_ref, kb_ref, vb_ref, lfb_ref), ob_ref, stb_ref, True)):
        q_r, k_r, v_r, lf_r = refs
        outs = []
        for h in range(HG_HEADS):
            sl = slice(h * HG_DK, (h + 1) * HG_DK)
            outs.append(_gla_chunk(q_r[:, sl], k_r[:, sl], v_r[:, sl], lf_r[:, sl],
                                   st_ref.at[h], rev))
        o_ref[...] = jnp.concatenate(outs, axis=1)


def _hgrn(hq, kf, lff, kb, lfb, hi, batch, seq_len):
    c = HG_CHUNK
    nc = seq_len // c
    fwd = pl.BlockSpec((c, HG_WIDTH), lambda b, i: (b * nc + i, 0))
    bwd = pl.BlockSpec((c, HG_WIDTH), lambda b, i: (b * nc + (nc - 1 - i), 0))
    out = jax.ShapeDtypeStruct((batch * seq_len, HG_WIDTH), F32)
    return pl.pallas_call(
        _hgrn_kernel,
        grid=(batch, nc),
        in_specs=[fwd, fwd, fwd, fwd, bwd, bwd, bwd, bwd],
        out_specs=[fwd, bwd],
        out_shape=[out, out],
        scratch_shapes=[pltpu.VMEM((HG_HEADS, HG_DV, HG_DK), F32),
                        pltpu.VMEM((HG_HEADS, HG_DV, HG_DK), F32)],
        compiler_params=_cparams(2),
        name="hgrn_scan",
    )(hq, kf, lff, hi, hq, kb, lfb, hi)


def _merge_kernel(x_ref, att_ref, of_ref, ob_ref, hg_ref, g_ref, on_ref, wua_ref, wuh_ref,
                  wo_ref, nf_ref, x1_out, hf_out):
    o = of_ref[...] + ob_ref[...]
    gate = hg_ref[...]
    parts = []
    for h in range(HG_HEADS):
        sl = slice(h * HG_DV, (h + 1) * HG_DV)
        parts.append(_rms(o[:, sl], on_ref[...]) * gate[:, sl])
    hn = jnp.concatenate(parts, axis=1).astype(BF16)
    y_hg = _dot(hn, wuh_ref[...])
    y_att = _dot(att_ref[...], wua_ref[...])
    g = g_ref[...]
    merged = g[:, :D_MODEL] * y_att + g[:, D_MODEL:] * y_hg
    x1 = x_ref[...] + _dot(merged.astype(BF16), wo_ref[...])
    x1_out[...] = x1
    hf_out[...] = _rms(x1, nf_ref[...])


def _merge(x2, att, o_f, o_b, hg_s, g, out_norm, w_up_att, w_up_hg, w_out, norm_ffn):
    t_total = x2.shape[0]
    tm = TM_PROJ

    def row_spec(width):
        return pl.BlockSpec((tm, width), lambda i: (i, 0))

    out = jax.ShapeDtypeStruct((t_total, D_MODEL), F32)
    return pl.pallas_call(
        _merge_kernel,
        grid=(t_total // tm,),
        in_specs=[row_spec(D_MODEL), row_spec(ATT_WIDTH), row_spec(HG_WIDTH), row_spec(HG_WIDTH),
                  row_spec(HG_WIDTH), row_spec(N_BRANCH * D_MODEL), _const_spec((1, HG_DV)),
                  _const_spec((ATT_WIDTH, D_MODEL)), _const_spec((HG_WIDTH, D_MODEL)),
                  _const_spec((D_MODEL, D_MODEL)), _const_spec((1, D_MODEL))],
        out_specs=[row_spec(D_MODEL), row_spec(D_MODEL)],
        out_shape=[out, out],
        compiler_params=_cparams(1),
        name="merge",
    )(x2, att, o_f, o_b, hg_s, g, out_norm.reshape(1, HG_DV), w_up_att.astype(BF16),
      w_up_hg.astype(BF16), w_out.astype(BF16), norm_ffn.reshape(1, D_MODEL))


def _topk_rows(s, k, key):
    big = jnp.float32(2.0 ** 30)
    vals, picks = [], []
    for _ in range(k):
        m = jnp.max(s, axis=0, keepdims=True)
        kmin = jnp.min(jnp.where(s == m, key, big), axis=0, keepdims=True)
        vals.append(m)
        picks.append(kmin)
        s = jnp.where(key == kmin, -jnp.inf, s)
    return jnp.concatenate(vals, axis=0), jnp.concatenate(picks, axis=0)


def _peer_topk_kernel(hf_ref, wqt_ref, sk_ref, idx_out, g_out):
    qt = _dot_nt(wqt_ref[...], hf_ref[...].astype(BF16)).astype(BF16)
    kk = PEER_TOPK
    half = PEER_DKEY // 2
    tokens = hf_ref.shape[0]
    key_row = lax.broadcasted_iota(I32, (PEER_NKEYS, tokens), 0).astype(F32)
    pos_scaled = (lax.broadcasted_iota(I32, (kk * kk, tokens), 0) * PEER_N_EXPERTS).astype(F32)
    ids = []
    for h in range(PEER_HEADS):
        tops = []
        for p in range(2):
            gi = h * 2 + p
            s = _dot(sk_ref[gi], qt[gi * half:(gi + 1) * half])
            tops.append(_topk_rows(s, kk, key_row))
        (v1, i1), (v2, i2) = tops
        cand = jnp.concatenate([v1[a:a + 1] + v2 for a in range(kk)], axis=0)
        cand_idx = jnp.concatenate([i1[a:a + 1] * PEER_NKEYS + i2 for a in range(kk)], axis=0)
        top_s, key = _topk_rows(cand, kk, pos_scaled + cand_idx)
        idx = key - jnp.floor(key * (1.0 / PEER_N_EXPERTS)) * PEER_N_EXPERTS
        e = jnp.exp(top_s - jnp.max(top_s, axis=0, keepdims=True))
        g_out[h * kk:(h + 1) * kk, :] = e / jnp.sum(e, axis=0, keepdims=True)
        ids.append(idx.astype(I32))
    idx_out[...] = jnp.concatenate(ids, axis=0).T


def _peer_topk(hf, peer_wq, peer_subkeys):
    t_total = hf.shape[0]
    tm = TM_PEER
    wqt = peer_wq.T.astype(BF16)
    sk = peer_subkeys.reshape(PEER_HEADS * 2, PEER_NKEYS, PEER_DKEY // 2).astype(BF16)
    return pl.pallas_call(
        _peer_topk_kernel,
        grid=(t_total // tm,),
        in_specs=[pl.BlockSpec((tm, D_MODEL), lambda i: (i, 0)),
                  _const_spec((PEER_HEADS * PEER_DKEY, D_MODEL)),
                  _const_spec((PEER_HEADS * 2, PEER_NKEYS, PEER_DKEY // 2))],
        out_specs=[pl.BlockSpec((tm, PEER_PAIRS), lambda i: (i, 0)),
                   pl.BlockSpec((PEER_PAIRS, tm), lambda i: (0, i))],
        out_shape=[jax.ShapeDtypeStruct((t_total, PEER_PAIRS), I32),
                   jax.ShapeDtypeStruct((PEER_PAIRS, t_total), F32)],
        compiler_params=_cparams(1),
        name="peer_topk",
    )(hf, wqt, sk)


HALF_ROWS = 4
ROW_MASK_HI = 0xFFFF0000


def _pack_expert_rows(tab):
    e, d = tab.shape
    bits = lax.bitcast_convert_type(tab.astype(BF16), jnp.uint16).astype(U32)
    return (bits[:, :d // 2] | (bits[:, d // 2:] << 16)).reshape(e, HALF_ROWS, LANES)


def _two_rows(tab_ref, idx_ref, pa, pb, t):
    return jnp.concatenate([tab_ref[idx_ref[t, pa]], tab_ref[idx_ref[t, pb]]], axis=0)


def _unpack_lo(w):
    return lax.bitcast_convert_type(w << 16, F32)


def _unpack_hi(w):
    return lax.bitcast_convert_type(w & jnp.uint32(ROW_MASK_HI), F32)


def _row_halves(xrow):
    lo = [xrow[:, c * LANES:(c + 1) * LANES] for c in range(HALF_ROWS)]
    hi = [xrow[:, (HALF_ROWS + c) * LANES:(HALF_ROWS + c + 1) * LANES] for c in range(HALF_ROWS)]
    return jnp.concatenate(lo + lo, axis=0), jnp.concatenate(hi + hi, axis=0)


def _peer_act_kernel(idx_ref, g_ref, hf_ref, tab_ref, w_out):
    sub = lax.broadcasted_iota(I32, (2 * HALF_ROWS, LANES), 0)
    low2 = (sub % 4) < 2
    even = (sub % 2) < 1
    lane = lax.broadcasted_iota(I32, (PEER_PAIRS, TM_PEER), 1)

    def fold2(x, y):
        return jnp.where(low2, x + pltpu.roll(x, 6, axis=0), y + pltpu.roll(y, 2, axis=0))

    def fold1(x, y):
        return jnp.where(even, x + pltpu.roll(x, 7, axis=0), y + pltpu.roll(y, 1, axis=0))

    def dots(t):
        x_lo, x_hi = _row_halves(hf_ref[pl.ds(t, 1), :])

        def prod(pa, pb):
            w = _two_rows(tab_ref, idx_ref, pa, pb, t)
            return _unpack_lo(w) * x_lo + _unpack_hi(w) * x_hi

        a_parts = []
        for grp in range(PEER_PAIRS // 8):
            b = grp * 8
            r = fold1(fold2(prod(b, b + 4), prod(b + 2, b + 6)),
                      fold2(prod(b + 1, b + 5), prod(b + 3, b + 7)))
            a_parts.append(jnp.sum(r, axis=1, keepdims=True))
        return jnp.concatenate(a_parts, axis=0)

    def token_pair(i, a_acc):
        t0 = 2 * i
        a0 = dots(t0)
        a1 = dots(t0 + 1)
        return jnp.where(lane == t0 + 1, a1, jnp.where(lane == t0, a0, a_acc))

    a_all = lax.fori_loop(0, TM_PEER // 2, token_pair, jnp.zeros((PEER_PAIRS, TM_PEER), F32))
    w_out[...] = jax.nn.gelu(a_all) * g_ref[...]


def _peer_act(idx_t, g_t, hf, peer_u):
    t_total = hf.shape[0]
    tm = TM_PEER
    return pl.pallas_call(
        _peer_act_kernel,
        grid=(t_total // tm,),
        in_specs=[pl.BlockSpec((tm, PEER_PAIRS), lambda i: (i, 0), memory_space=pltpu.SMEM),
                  pl.BlockSpec((PEER_PAIRS, tm), lambda i: (0, i)),
                  pl.BlockSpec((tm, D_MODEL), lambda i: (i, 0)),
                  pl.BlockSpec(memory_space=pltpu.VMEM)],
        out_specs=pl.BlockSpec((PEER_PAIRS, tm), lambda i: (0, i)),
        out_shape=jax.ShapeDtypeStruct((PEER_PAIRS, t_total), F32),
        compiler_params=_cparams(1),
        name="peer_act",
    )(idx_t, g_t, hf, _pack_expert_rows(peer_u))


def _peer_mix_kernel(idx_ref, w_ref, x1_ref, p_ref, np_ref, wg_ref, wp_ref, tab_ref, out_ref,
                     x2_scr):
    low4 = lax.broadcasted_iota(I32, (2 * HALF_ROWS, LANES), 0) < HALF_ROWS
    lane = lax.broadcasted_iota(I32, (PEER_PAIRS, TM_PEER), 1)
    tile = (2 * HALF_ROWS, LANES)

    def token(t):
        wcol = jnp.sum(jnp.where(lane == t, w_ref[...], 0.0), axis=1, keepdims=True)
        wcol = jnp.broadcast_to(wcol, (PEER_PAIRS, LANES))
        acc_lo = [jnp.zeros(tile, F32) for _ in range(2)]
        acc_hi = [jnp.zeros(tile, F32) for _ in range(2)]
        for q in range(PEER_PAIRS // 2):
            w = _two_rows(tab_ref, idx_ref, 2 * q, 2 * q + 1, t)
            wgt = jnp.where(low4, jnp.broadcast_to(wcol[2 * q:2 * q + 1], tile),
                            jnp.broadcast_to(wcol[2 * q + 1:2 * q + 2], tile))
            acc_lo[q % 2] = acc_lo[q % 2] + wgt * _unpack_lo(w)
            acc_hi[q % 2] = acc_hi[q % 2] + wgt * _unpack_hi(w)
        lo = acc_lo[0] + acc_lo[1]
        hi = acc_hi[0] + acc_hi[1]
        lo = lo[:HALF_ROWS] + lo[HALF_ROWS:]
        hi = hi[:HALF_ROWS] + hi[HALF_ROWS:]
        row = jnp.concatenate([lo[c:c + 1] for c in range(HALF_ROWS)]
                              + [hi[c:c + 1] for c in range(HALF_ROWS)], axis=1)
        x2_scr[pl.ds(t, 1), :] = x1_ref[pl.ds(t, 1), :] + row

    def token_pair(i, carry):
        token(2 * i)
        token(2 * i + 1)
        return carry

    lax.fori_loop(0, TM_PEER // 2, token_pair, 0)
    x2 = x2_scr[...]
    gate = jax.nn.sigmoid(_dot(_rms(x2, np_ref[...]).astype(BF16), wg_ref[...]))
    out_ref[...] = x2 + gate * _dot(p_ref[...].astype(BF16), wp_ref[...])


def _peer_mix_ple(idx_t, w_t, x1, peer_v, p2, norm_ple, ple_gate, ple_proj):
    t_total = x1.shape[0]
    tm = TM_PEER
    return pl.pallas_call(
        _peer_mix_kernel,
        grid=(t_total // tm,),
        in_specs=[pl.BlockSpec((tm, PEER_PAIRS), lambda i: (i, 0), memory_space=pltpu.SMEM),
                  pl.BlockSpec((PEER_PAIRS, tm), lambda i: (0, i)),
                  pl.BlockSpec((tm, D_MODEL), lambda i: (i, 0)),
                  pl.BlockSpec((tm, PLE_DIM), lambda i: (i, 0)),
                  _const_spec((1, D_MODEL)), _const_spec((D_MODEL, D_MODEL)),
                  _const_spec((PLE_DIM, D_MODEL)),
                  pl.BlockSpec(memory_space=pltpu.VMEM)],
        out_specs=pl.BlockSpec((tm, D_MODEL), lambda i: (i, 0)),
        out_shape=jax.ShapeDtypeStruct((t_total, D_MODEL), F32),
        scratch_shapes=[pltpu.VMEM((tm, D_MODEL), F32)],
        compiler_params=_cparams(1),
        name="peer_mix",
    )(idx_t, w_t, x1, p2, norm_ple.reshape(1, D_MODEL), ple_gate.astype(BF16),
      ple_proj.astype(BF16), _pack_expert_rows(peer_v))


def kernel(x, p, norm_mix, w_in, q_norm, k_norm, hg_lb_raw, hg_out_norm, w_up_att, w_up_hg, w_out,
           norm_ffn, peer_wq, peer_subkeys, peer_u, peer_v, norm_ple, ple_gate, ple_proj):
    batch, seq_len, d = x.shape
    depth = p.shape[0]
    assert d == D_MODEL and depth == 1 and hg_lb_raw.shape == (2, 2, HG_WIDTH)
    assert seq_len % TM_PROJ == 0 and seq_len % HG_CHUNK == 0 and (batch * seq_len) % TM_PEER == 0
    x2 = x.reshape(batch * seq_len, d)
    (q, k, v, hq, kf, lff, kb, lfb, hi, hg_s, g) = _inproj(
        x2, seq_len, norm_mix[0], w_in[0], q_norm[0], k_norm[0], hg_lb_raw)
    att = _attention(q, k, v, batch, seq_len)
    o_f, o_b = _hgrn(hq, kf, lff, kb, lfb, hi, batch, seq_len)
    x1, hf = _merge(x2, att, o_f, o_b, hg_s, g, hg_out_norm[0], w_up_att[0], w_up_hg[0], w_out[0],
                    norm_ffn[0])
    idx_t, g_t = _peer_topk(hf, peer_wq[0], peer_subkeys[0])
    w_t = _peer_act(idx_t, g_t, hf, peer_u[0])
    out = _peer_mix_ple(idx_t, w_t, x1, peer_v[0], p[0].reshape(batch * seq_len, PLE_DIM),
                        norm_ple[0], ple_gate[0], ple_proj[0])
    return out.reshape(batch, seq_len, d)
```

```python
import functools
import math

import jax
import jax.numpy as jnp
import numpy as np
from jax import lax
from jax.experimental import pallas as pl
from jax.experimental.pallas import tpu as pltpu

F32 = jnp.float32
BF16 = jnp.bfloat16
I32 = jnp.int32
U32 = jnp.uint32

D_MODEL = 1024
GRID_W = 64
ATT_HEADS = 8
ATT_KV_HEADS = 2
ATT_GROUP = ATT_HEADS // ATT_KV_HEADS
ATT_HEAD_DIM = 64
ATT_WIDTH = ATT_HEADS * ATT_HEAD_DIM
ATT_KV_WIDTH = ATT_KV_HEADS * ATT_HEAD_DIM
ROPE_THETA = 10000.0
HG_HEADS = 4
HG_DK = 128
HG_DV = 128
HG_WIDTH = HG_HEADS * HG_DK
HG_CHUNK = 64
HG_SUB = 16
N_BRANCH = 2
PEER_HEADS = 8
PEER_NKEYS = 128
PEER_DKEY = 256
PEER_TOPK = 16
PEER_PAIRS = PEER_HEADS * PEER_TOPK
PEER_N_EXPERTS = PEER_NKEYS * PEER_NKEYS
PLE_DIM = 256
EPS = 1e-6

LANES = 128
VMEM_LIMIT_BYTES = 56 * 1024 * 1024

TM_PROJ = 256
TQ_ATT = 256
TM_PEER = 128
PEER_TRIP = 8


def _cparams(n_axes):
    return pltpu.CompilerParams(
        dimension_semantics=("arbitrary",) * n_axes,
        vmem_limit_bytes=VMEM_LIMIT_BYTES,
    )


def _const_spec(shape):
    nd = len(shape)
    return pl.BlockSpec(shape, lambda *_: (0,) * nd)


def _dot(a, b):
    return jnp.dot(a, b, preferred_element_type=F32)


def _dot_nt(a, b):
    return lax.dot_general(a, b, (((1,), (1,)), ((), ())), preferred_element_type=F32)


def _dot_tn(a, b):
    return lax.dot_general(a, b, (((0,), (0,)), ((), ())), preferred_element_type=F32)


def _split3(x):
    hi = x.astype(BF16)
    r = x - hi.astype(F32)
    mid = r.astype(BF16)
    lo = (r - mid.astype(F32)).astype(BF16)
    return hi, mid, lo


def _rms(x, gain):
    ms = jnp.mean(x * x, axis=-1, keepdims=True)
    return x * lax.rsqrt(ms + EPS) * gain


def _head_rms_rope(a, gain, m_blk, cos, sin_signed, first_half):
    sq = a * a
    hi = sq.astype(BF16)
    lo = (sq - hi.astype(F32)).astype(BF16)
    ms = _dot(hi, m_blk) + _dot(lo, m_blk)
    y = a * lax.rsqrt(ms + EPS) * gain
    w = y.shape[-1]
    nxt = pltpu.roll(y, w - HG_SUB, axis=1)
    prv = pltpu.roll(y, HG_SUB, axis=1)
    partner = jnp.where(first_half, nxt, prv)
    return y * cos + partner * sin_signed


def _inproj_kernel(x_ref, nm_ref, wq_ref, wk_ref, wv_ref, whq_ref, wff_ref, wfb_ref, whi_ref,
                   whg_ref, wg_ref, qg_ref, kg_ref, cos_ref, sin_ref, m_ref, lbraw_ref,
                   q_out, k_out, v_out, hq_out, kf_out, lff_out, kb_out, lfb_out, hi_out,
                   hg_out, g_out):
    h = _rms(x_ref[...], nm_ref[...]).astype(BF16)

    def proj(w_ref):
        return _dot(h, w_ref[...])

    cos = cos_ref[...]
    sin = sin_ref[...]
    lane = lax.broadcasted_iota(I32, (1, LANES), 1)
    first_half = (lane % 32) < 16

    def tile_lanes(t, reps):
        return jnp.concatenate([t] * reps, axis=1) if reps > 1 else t

    rq = ATT_WIDTH // LANES
    q = _head_rms_rope(proj(wq_ref), qg_ref[...], m_ref[...], tile_lanes(cos, rq),
                       tile_lanes(sin, rq), tile_lanes(first_half, rq))
    q_out[...] = (q * (ATT_HEAD_DIM ** -0.5)).astype(BF16)
    k = _head_rms_rope(proj(wk_ref), kg_ref[...], m_ref[0:ATT_KV_WIDTH, 0:ATT_KV_WIDTH],
                       cos, sin, first_half)
    k_out[...] = k.astype(BF16)
    v_out[...] = proj(wv_ref).astype(BF16)

    hq = proj(whq_ref)
    hq_out[...] = hq * jax.nn.sigmoid(hq)

    r = lbraw_ref[...]
    for d, (w_ref, k_ref, lf_ref) in enumerate(((wff_ref, kf_out, lff_out),
                                                 (wfb_ref, kb_out, lfb_out))):
        a0 = r[d:d + 1]
        a1 = r[2 + d:3 + d]
        mx = jnp.maximum(a0, a1)
        e0 = jnp.exp(a0 - mx)
        e1 = jnp.exp(a1 - mx)
        lb = e0 / (e0 + e1)
        f = lb + (1.0 - lb) * jax.nn.sigmoid(proj(w_ref))
        k_ref[...] = 1.0 - f
        lf_ref[...] = jnp.log(f)

    hi_out[...] = proj(whi_ref)
    hg = proj(whg_ref)
    hg_out[...] = hg * jax.nn.sigmoid(hg)
    g_out[...] = jax.nn.sigmoid(proj(wg_ref))


def _rope_tables(seq_len):
    lane = np.arange(LANES)
    axis = (lane % ATT_HEAD_DIM) // 32
    j = lane % 16
    sign = np.where((lane % 32) < 16, -1.0, 1.0).astype(np.float32)
    rot_half = ATT_HEAD_DIM // 4
    inv = ROPE_THETA ** (-jnp.arange(rot_half, dtype=F32) / rot_half)
    t = jnp.arange(seq_len)
    pos = jnp.stack([(t // GRID_W).astype(F32), (t % GRID_W).astype(F32)], axis=1)
    ang = pos[:, axis] * inv[j][None, :]
    return jnp.cos(ang), jnp.sin(ang) * sign[None, :]


def _inproj(x2, seq_len, norm_mix, w_in, q_norm, k_norm, lb_raw):
    t_total = x2.shape[0]
    tm = TM_PROJ
    n_tiles = t_total // tm
    tiles_per_seq = seq_len // tm
    splits = (ATT_WIDTH, ATT_KV_WIDTH, ATT_KV_WIDTH, HG_WIDTH, HG_WIDTH, HG_WIDTH, HG_WIDTH,
              HG_WIDTH, N_BRANCH * D_MODEL)
    offs = np.cumsum((0,) + splits)
    w_bf = w_in.astype(BF16)
    ws = [w_bf[:, offs[i]:offs[i + 1]] for i in range(len(splits))]
    cos, sin = _rope_tables(seq_len)
    grp = np.arange(ATT_WIDTH) // ATT_HEAD_DIM
    m_blk = jnp.asarray((grp[:, None] == grp[None, :]).astype(np.float32) / ATT_HEAD_DIM, BF16)
    qg = jnp.tile(q_norm.astype(F32), ATT_HEADS)[None, :]
    kg = jnp.tile(k_norm.astype(F32), ATT_KV_HEADS)[None, :]

    def row_spec(width):
        return pl.BlockSpec((tm, width), lambda i: (i, 0))

    in_specs = [row_spec(D_MODEL), _const_spec((1, D_MODEL))]
    in_specs += [_const_spec((D_MODEL, s)) for s in splits]
    in_specs += [_const_spec((1, ATT_WIDTH)), _const_spec((1, ATT_KV_WIDTH)),
                 pl.BlockSpec((tm, LANES), lambda i: (i % tiles_per_seq, 0)),
                 pl.BlockSpec((tm, LANES), lambda i: (i % tiles_per_seq, 0)),
                 _const_spec((ATT_WIDTH, ATT_WIDTH)), _const_spec((4, HG_WIDTH))]
    out_widths = (ATT_WIDTH, ATT_KV_WIDTH, ATT_KV_WIDTH) + (HG_WIDTH,) * 7 + (N_BRANCH * D_MODEL,)
    out_dtypes = (BF16, BF16, BF16) + (F32,) * 8
    return pl.pallas_call(
        _inproj_kernel,
        grid=(n_tiles,),
        in_specs=in_specs,
        out_specs=[row_spec(w) for w in out_widths],
        out_shape=[jax.ShapeDtypeStruct((t_total, w), dt) for w, dt in zip(out_widths, out_dtypes)],
        compiler_params=_cparams(1),
        name="inproj",
    )(x2, norm_mix.reshape(1, D_MODEL), *ws, qg, kg, cos, sin, m_blk, lb_raw.reshape(4, HG_WIDTH))


def _attention_kernel(q_ref, k_ref, v_ref, o_ref):
    q = q_ref[...]
    k = k_ref[...]
    v = v_ref[...]
    outs = []
    for kvh in range(ATT_KV_HEADS):
        kh = k[:, kvh * ATT_HEAD_DIM:(kvh + 1) * ATT_HEAD_DIM]
        vh = v[:, kvh * ATT_HEAD_DIM:(kvh + 1) * ATT_HEAD_DIM]
        for g in range(ATT_GROUP):
            hd = kvh * ATT_GROUP + g
            qh = q[:, hd * ATT_HEAD_DIM:(hd + 1) * ATT_HEAD_DIM]
            s = _dot_nt(qh, kh)
            m = jnp.max(s, axis=-1, keepdims=True)
            p = jnp.exp(s - m)
            l = jnp.sum(p, axis=-1, keepdims=True)
            outs.append(_dot(p.astype(BF16), vh) / l)
    o_ref[...] = jnp.concatenate(outs, axis=1).astype(BF16)


def _attention(q, k, v, batch, seq_len):
    tq = TQ_ATT
    nq = seq_len // tq
    return pl.pallas_call(
        _attention_kernel,
        grid=(batch, nq),
        in_specs=[pl.BlockSpec((tq, ATT_WIDTH), lambda b, i: (b * nq + i, 0)),
                  pl.BlockSpec((seq_len, ATT_KV_WIDTH), lambda b, i: (b, 0)),
                  pl.BlockSpec((seq_len, ATT_KV_WIDTH), lambda b, i: (b, 0))],
        out_specs=pl.BlockSpec((tq, ATT_WIDTH), lambda b, i: (b * nq + i, 0)),
        out_shape=jax.ShapeDtypeStruct((batch * seq_len, ATT_WIDTH), BF16),
        compiler_params=_cparams(2),
        name="attention",
    )(q, k, v)


def _gla_chunk(q, k, v, lf, st_ref, reverse):
    c = HG_CHUNK
    row = lax.broadcasted_iota(I32, (c, c), 0)
    col = lax.broadcasted_iota(I32, (c, c), 1)
    tri = (col >= row) if reverse else (col <= row)
    tri = tri.astype(BF16)
    hi, mid, lo = _split3(lf)
    b = _dot(tri, hi) + _dot(tri, mid) + _dot(tri, lo)
    edge = 0 if reverse else c - 1
    b_all = b[edge:edge + 1]

    st = st_ref[...]
    o = _dot_nt((q * jnp.exp(b)).astype(BF16), st.astype(BF16))
    k_dec = k * jnp.exp(b_all - b)
    st_ref[...] = st * jnp.exp(b_all) + _dot_tn(v.astype(BF16), k_dec.astype(BF16))

    if reverse:
        blocks = ((0, 32, 32, 64, 32), (0, 16, 16, 32, 16), (32, 48, 48, 64, 48))
    else:
        blocks = ((32, 64, 0, 32, 31), (16, 32, 0, 16, 15), (48, 64, 32, 48, 47))
    pieces = {}
    for r0, r1, c0, c1, ref in blocks:
        b_ref = b[ref:ref + 1]
        qs = q[r0:r1] * jnp.exp(b[r0:r1] - b_ref)
        ks = k[c0:c1] * jnp.exp(b_ref - b[c0:c1])
        a = _dot_nt(qs.astype(BF16), ks.astype(BF16))
        pieces.setdefault(r0, []).append((r1, _dot(a.astype(BF16), v[c0:c1].astype(BF16))))

    out_rows = []
    sub = HG_SUB
    trow = lax.broadcasted_iota(I32, (sub, 1), 0)
    for i in range(c // sub):
        lo_r = i * sub
        qb = q[lo_r:lo_r + sub]
        bb = b[lo_r:lo_r + sub]
        acc = jnp.zeros((sub, HG_DV), F32)
        for s in range(sub):
            keep = (trow <= s) if reverse else (trow >= s)
            dec = jnp.exp(jnp.where(keep, bb - b[lo_r + s:lo_r + s + 1], -jnp.inf))
            a_col = jnp.sum(qb * k[lo_r + s:lo_r + s + 1] * dec, axis=1, keepdims=True)
            acc = acc + a_col * v[lo_r + s:lo_r + s + 1]
        out_rows.append(acc)
    intra = jnp.concatenate(out_rows, axis=0)
    for r0, plist in pieces.items():
        for r1, val in plist:
            pad_top = r0
            pad_bot = c - r1
            parts = []
            if pad_top:
                parts.append(jnp.zeros((pad_top, HG_DV), F32))
            parts.append(val)
            if pad_bot:
                parts.append(jnp.zeros((pad_bot, HG_DV), F32))
            intra = intra + jnp.concatenate(parts, axis=0)
    return o + intra


def _hgrn_kernel(qf_ref, kf_ref, lff_ref, vf_ref, qb_ref, kb_ref, lfb_ref, vb_ref,
                 of_ref, ob_ref, stf_ref, stb_ref):
    @pl.when(pl.program_id(1) == 0)
    def _():
        stf_ref[...] = jnp.zeros_like(stf_ref)
        stb_ref[...] = jnp.zeros_like(stb_ref)

    for refs, o_ref, st_ref, rev in (((qf_ref, kf_ref, vf_ref, lff_ref), of_ref, stf_ref, False),
                                     ((qb_ref, kb_ref, vb_ref, lfb_ref), ob_ref, stb_ref, True)):
        q_r, k_r, v_r, lf_r = refs
        outs = []
        for h in range(HG_HEADS):
            sl = slice(h * HG_DK, (h + 1) * HG_DK)
            outs.append(_gla_chunk(q_r[:, sl], k_r[:, sl], v_r[:, sl], lf_r[:, sl],
                                   st_ref.at[h], rev))
        o_ref[...] = jnp.concatenate(outs, axis=1)


def _hgrn(hq, kf, lff, kb, lfb, hi, batch, seq_len):
    c = HG_CHUNK
    nc = seq_len // c
    fwd = pl.BlockSpec((c, HG_WIDTH), lambda b, i: (b * nc + i, 0))
    bwd = pl.BlockSpec((c, HG_WIDTH), lambda b, i: (b * nc + (nc - 1 - i), 0))
    out = jax.ShapeDtypeStruct((batch * seq_len, HG_WIDTH), F32)
    return pl.pallas_call(
        _hgrn_kernel,
        grid=(batch, nc),
        in_specs=[fwd, fwd, fwd, fwd, bwd, bwd, bwd, bwd],
        out_specs=[fwd, bwd],
        out_shape=[out, out],
        scratch_shapes=[pltpu.VMEM((HG_HEADS, HG_DV, HG_DK), F32),
                        pltpu.VMEM((HG_HEADS, HG_DV, HG_DK), F32)],
        compiler_params=_cparams(2),
        name="hgrn_scan",
    )(hq, kf, lff, hi, hq, kb, lfb, hi)


def _merge_kernel(x_ref, att_ref, of_ref, ob_ref, hg_ref, g_ref, on_ref, wua_ref, wuh_ref,
                  wo_ref, nf_ref, x1_out, hf_out):
    o = of_ref[...] + ob_ref[...]
    gate = hg_ref[...]
    parts = []
    for h in range(HG_HEADS):
        sl = slice(h * HG_DV, (h + 1) * HG_DV)
        parts.append(_rms(o[:, sl], on_ref[...]) * gate[:, sl])
    hn = jnp.concatenate(parts, axis=1).astype(BF16)
    y_hg = _dot(hn, wuh_ref[...])
    y_att = _dot(att_ref[...], wua_ref[...])
    g = g_ref[...]
    merged = g[:, :D_MODEL] * y_att + g[:, D_MODEL:] * y_hg
    x1 = x_ref[...] + _dot(merged.astype(BF16), wo_ref[...])
    x1_out[...] = x1
    hf_out[...] = _rms(x1, nf_ref[...])


def _merge(x2, att, o_f, o_b, hg_s, g, out_norm, w_up_att, w_up_hg, w_out, norm_ffn):
    t_total = x2.shape[0]
    tm = TM_PROJ

    def row_spec(width):
        return pl.BlockSpec((tm, width), lambda i: (i, 0))

    out = jax.ShapeDtypeStruct((t_total, D_MODEL), F32)
    return pl.pallas_call(
        _merge_kernel,
        grid=(t_total // tm,),
        in_specs=[row_spec(D_MODEL), row_spec(ATT_WIDTH), row_spec(HG_WIDTH), row_spec(HG_WIDTH),
                  row_spec(HG_WIDTH), row_spec(N_BRANCH * D_MODEL), _const_spec((1, HG_DV)),
                  _const_spec((ATT_WIDTH, D_MODEL)), _const_spec((HG_WIDTH, D_MODEL)),
                  _const_spec((D_MODEL, D_MODEL)), _const_spec((1, D_MODEL))],
        out_specs=[row_spec(D_MODEL), row_spec(D_MODEL)],
        out_shape=[out, out],
        compiler_params=_cparams(1),
        name="merge",
    )(x2, att, o_f, o_b, hg_s, g, out_norm.reshape(1, HG_DV), w_up_att.astype(BF16),
      w_up_hg.astype(BF16), w_out.astype(BF16), norm_ffn.reshape(1, D_MODEL))


def _topk_rows(s, k, key):
    big = jnp.float32(2.0 ** 30)
    vals, picks = [], []
    for _ in range(k):
        m = jnp.max(s, axis=0, keepdims=True)
        kmin = jnp.min(jnp.where(s == m, key, big), axis=0, keepdims=True)
        vals.append(m)
        picks.append(kmin)
        s = jnp.where(key == kmin, -jnp.inf, s)
    return jnp.concatenate(vals, axis=0), jnp.concatenate(picks, axis=0)


def _peer_topk_kernel(hf_ref, wqt_ref, sk_ref, idx_out, g_out):
    qt = _dot_nt(wqt_ref[...], hf_ref[...].astype(BF16)).astype(BF16)
    kk = PEER_TOPK
    half = PEER_DKEY // 2
    tokens = hf_ref.shape[0]
    key_row = lax.broadcasted_iota(I32, (PEER_NKEYS, tokens), 0).astype(F32)
    row16 = key_row[:kk]
    ids = []
    for h in range(PEER_HEADS):
        tops = []
        for p in range(2):
            gi = h * 2 + p
            s = _dot(sk_ref[gi], qt[gi * half:(gi + 1) * half])
            tops.append(_topk_rows(s, kk, key_row))
        (v1, i1), (v2, i2) = tops
        vals, keys = [], []
        for a, nb in ((0, 16), (1, 8), (2, 8), (3, 8)):
            vals.append(v1[a:a + 1] + v2[:nb])
            keys.append((row16[:nb] + a * kk) * PEER_N_EXPERTS + (i1[a:a + 1] * PEER_NKEYS + i2[:nb]))
        for b, na in ((0, 16), (1, 8), (2, 8)):
            vals.append(jnp.where(row16[:na] >= 4, v1[:na] + v2[b:b + 1], -jnp.inf))
            keys.append((row16[:na] * kk + b) * PEER_N_EXPERTS + (i1[:na] * PEER_NKEYS + i2[b:b + 1]))
        top_s, key = _topk_rows(jnp.concatenate(vals, axis=0), kk, jnp.concatenate(keys, axis=0))
        idx = key - jnp.floor(key * (1.0 / PEER_N_EXPERTS)) * PEER_N_EXPERTS
        e = jnp.exp(top_s - jnp.max(top_s, axis=0, keepdims=True))
        g_out[h * kk:(h + 1) * kk, :] = e / jnp.sum(e, axis=0, keepdims=True)
        ids.append(idx.astype(I32))
    idx_out[...] = jnp.concatenate(ids, axis=0).T


def _peer_topk(hf, peer_wq, peer_subkeys):
    t_total = hf.shape[0]
    tm = TM_PEER
    wqt = peer_wq.T.astype(BF16)
    sk = peer_subkeys.reshape(PEER_HEADS * 2, PEER_NKEYS, PEER_DKEY // 2).astype(BF16)
    return pl.pallas_call(
        _peer_topk_kernel,
        grid=(t_total // tm,),
        in_specs=[pl.BlockSpec((tm, D_MODEL), lambda i: (i, 0)),
                  _const_spec((PEER_HEADS * PEER_DKEY, D_MODEL)),
                  _const_spec((PEER_HEADS * 2, PEER_NKEYS, PEER_DKEY // 2))],
        out_specs=[pl.BlockSpec((tm, PEER_PAIRS), lambda i: (i, 0)),
                   pl.BlockSpec((PEER_PAIRS, tm), lambda i: (0, i))],
        out_shape=[jax.ShapeDtypeStruct((t_total, PEER_PAIRS), I32),
                   jax.ShapeDtypeStruct((PEER_PAIRS, t_total), F32)],
        compiler_params=_cparams(1),
        name="peer_topk",
    )(hf, wqt, sk)


HALF_ROWS = 4
ROW_MASK_HI = 0xFFFF0000


def _pack_expert_rows(tab):
    e, d = tab.shape
    bits = lax.bitcast_convert_type(tab.astype(BF16), jnp.uint16).astype(U32)
    return (bits[:, :d // 2] | (bits[:, d // 2:] << 16)).reshape(e, HALF_ROWS, LANES)


def _two_rows(tab_ref, idx_ref, pa, pb, t):
    return jnp.concatenate([tab_ref[idx_ref[t, pa]], tab_ref[idx_ref[t, pb]]], axis=0)


def _unpack_lo(w):
    return lax.bitcast_convert_type(w << 16, F32)


def _unpack_hi(w):
    return lax.bitcast_convert_type(w & jnp.uint32(ROW_MASK_HI), F32)


def _row_halves(xrow):
    lo = [xrow[:, c * LANES:(c + 1) * LANES] for c in range(HALF_ROWS)]
    hi = [xrow[:, (HALF_ROWS + c) * LANES:(HALF_ROWS + c + 1) * LANES] for c in range(HALF_ROWS)]
    return jnp.concatenate(lo + lo, axis=0), jnp.concatenate(hi + hi, axis=0)


def _peer_act_kernel(idx_ref, g_ref, hf_ref, tab_ref, w_out):
    sub = lax.broadcasted_iota(I32, (2 * HALF_ROWS, LANES), 0)
    low2 = (sub % 4) < 2
    even = (sub % 2) < 1
    lane = lax.broadcasted_iota(I32, (PEER_PAIRS, TM_PEER), 1)

    def fold2(x, y):
        return jnp.where(low2, x + pltpu.roll(x, 6, axis=0), y + pltpu.roll(y, 2, axis=0))

    def fold1(x, y):
        return jnp.where(even, x + pltpu.roll(x, 7, axis=0), y + pltpu.roll(y, 1, axis=0))

    def dots(t):
        x_lo, x_hi = _row_halves(hf_ref[pl.ds(t, 1), :])

        def prod(pa, pb):
            w = _two_rows(tab_ref, idx_ref, pa, pb, t)
            return _unpack_lo(w) * x_lo + _unpack_hi(w) * x_hi

        a_parts = []
        for grp in range(PEER_PAIRS // 8):
            b = grp * 8
            r = fold1(fold2(prod(b, b + 4), prod(b + 2, b + 6)),
                      fold2(prod(b + 1, b + 5), prod(b + 3, b + 7)))
            a_parts.append(jnp.sum(r, axis=1, keepdims=True))
        return jnp.concatenate(a_parts, axis=0)

    def token_group(i, a_acc):
        t0 = PEER_TRIP * i
        cols = [dots(t0 + j) for j in range(PEER_TRIP)]
        for j in range(PEER_TRIP):
            a_acc = jnp.where(lane == t0 + j, cols[j], a_acc)
        return a_acc

    a_all = lax.fori_loop(0, TM_PEER // PEER_TRIP, token_group,
                          jnp.zeros((PEER_PAIRS, TM_PEER), F32))
    w_out[...] = jax.nn.gelu(a_all) * g_ref[...]


def _peer_act(idx_t, g_t, hf, peer_u):
    t_total = hf.shape[0]
    tm = TM_PEER
    return pl.pallas_call(
        _peer_act_kernel,
        grid=(t_total // tm,),
        in_specs=[pl.BlockSpec((tm, PEER_PAIRS), lambda i: (i, 0), memory_space=pltpu.SMEM),
                  pl.BlockSpec((PEER_PAIRS, tm), lambda i: (0, i)),
                  pl.BlockSpec((tm, D_MODEL), lambda i: (i, 0)),
                  pl.BlockSpec(memory_space=pltpu.VMEM)],
        out_specs=pl.BlockSpec((PEER_PAIRS, tm), lambda i: (0, i)),
        out_shape=jax.ShapeDtypeStruct((PEER_PAIRS, t_total), F32),
        compiler_params=_cparams(1),
        name="peer_act",
    )(idx_t, g_t, hf, _pack_expert_rows(peer_u))


def _peer_mix_kernel(idx_ref, w_ref, x1_ref, p_ref, np_ref, wg_ref, wp_ref, tab_ref, out_ref,
                     x2_scr):
    low4 = lax.broadcasted_iota(I32, (2 * HALF_ROWS, LANES), 0) < HALF_ROWS
    lane = lax.broadcasted_iota(I32, (PEER_PAIRS, TM_PEER), 1)
    tile = (2 * HALF_ROWS, LANES)

    def token(t):
        wcol = jnp.sum(jnp.where(lane == t, w_ref[...], 0.0), axis=1, keepdims=True)
        wcol = jnp.broadcast_to(wcol, (PEER_PAIRS, LANES))
        acc_lo = [jnp.zeros(tile, F32) for _ in range(2)]
        acc_hi = [jnp.zeros(tile, F32) for _ in range(2)]
        for q in range(PEER_PAIRS // 2):
            w = _two_rows(tab_ref, idx_ref, 2 * q, 2 * q + 1, t)
            wgt = jnp.where(low4, jnp.broadcast_to(wcol[2 * q:2 * q + 1], tile),
                            jnp.broadcast_to(wcol[2 * q + 1:2 * q + 2], tile))
            acc_lo[q % 2] = acc_lo[q % 2] + wgt * _unpack_lo(w)
            acc_hi[q % 2] = acc_hi[q % 2] + wgt * _unpack_hi(w)
        lo = acc_lo[0] + acc_lo[1]
        hi = acc_hi[0] + acc_hi[1]
        lo = lo[:HALF_ROWS] + lo[HALF_ROWS:]
        hi = hi[:HALF_ROWS] + hi[HALF_ROWS:]
        row = jnp.concatenate([lo[c:c + 1] for c in range(HALF_ROWS)]
                              + [hi[c:c + 1] for c in range(HALF_ROWS)], axis=1)
        x2_scr[pl.ds(t, 1), :] = x1_ref[pl.ds(t, 1), :] + row

    def token_group(i, carry):
        for j in range(PEER_TRIP):
            token(PEER_TRIP * i + j)
        return carry

    lax.fori_loop(0, TM_PEER // PEER_TRIP, token_group, 0)
    x2 = x2_scr[...]
    gate = jax.nn.sigmoid(_dot(_rms(x2, np_ref[...]).astype(BF16), wg_ref[...]))
    out_ref[...] = x2 + gate * _dot(p_ref[...].astype(BF16), wp_ref[...])


def _peer_mix_ple(idx_t, w_t, x1, peer_v, p2, norm_ple, ple_gate, ple_proj):
    t_total = x1.shape[0]
    tm = TM_PEER
    return pl.pallas_call(
        _peer_mix_kernel,
        grid=(t_total // tm,),
        in_specs=[pl.BlockSpec((tm, PEER_PAIRS), lambda i: (i, 0), memory_space=pltpu.SMEM),
                  pl.BlockSpec((PEER_PAIRS, tm), lambda i: (0, i)),
                  pl.BlockSpec((tm, D_MODEL), lambda i: (i, 0)),
                  pl.BlockSpec((tm, PLE_DIM), lambda i: (i, 0)),
                  _const_spec((1, D_MODEL)), _const_spec((D_MODEL, D_MODEL)),
                  _const_spec((PLE_DIM, D_MODEL)),
                  pl.BlockSpec(memory_space=pltpu.VMEM)],
        out_specs=pl.BlockSpec((tm, D_MODEL), lambda i: (i, 0)),
        out_shape=jax.ShapeDtypeStruct((t_total, D_MODEL), F32),
        scratch_shapes=[pltpu.VMEM((tm, D_MODEL), F32)],
        compiler_params=_cparams(1),
        name="peer_mix",
    )(idx_t, w_t, x1, p2, norm_ple.reshape(1, D_MODEL), ple_gate.astype(BF16),
      ple_proj.astype(BF16), _pack_expert_rows(peer_v))


def kernel(x, p, norm_mix, w_in, q_norm, k_norm, hg_lb_raw, hg_out_norm, w_up_att, w_up_hg, w_out,
           norm_ffn, peer_wq, peer_subkeys, peer_u, peer_v, norm_ple, ple_gate, ple_proj):
    batch, seq_len, d = x.shape
    depth = p.shape[0]
    assert d == D_MODEL and depth == 1 and hg_lb_raw.shape == (2, 2, HG_WIDTH)
    assert seq_len % TM_PROJ == 0 and seq_len % HG_CHUNK == 0 and (batch * seq_len) % TM_PEER == 0
    x2 = x.reshape(batch * seq_len, d)
    (q, k, v, hq, kf, lff, kb, lfb, hi, hg_s, g) = _inproj(
        x2, seq_len, norm_mix[0], w_in[0], q_norm[0], k_norm[0], hg_lb_raw)
    att = _attention(q, k, v, batch, seq_len)
    o_f, o_b = _hgrn(hq, kf, lff, kb, lfb, hi, batch, seq_len)
    x1, hf = _merge(x2, att, o_f, o_b, hg_s, g, hg_out_norm[0], w_up_att[0], w_up_hg[0], w_out[0],
                    norm_ffn[0])
    idx_t, g_t = _peer_topk(hf, peer_wq[0], peer_subkeys[0])
    w_t = _peer_act(idx_t, g_t, hf, peer_u[0])
    out = _peer_mix_ple(idx_t, w_t, x1, peer_v[0], p[0].reshape(batch * seq_len, PLE_DIM),
                        norm_ple[0], ple_gate[0], ple_proj[0])
    return out.reshape(batch, seq_len, d)
```

```python
import functools
import math

import jax
import jax.numpy as jnp
import numpy as np
from jax import lax
from jax.experimental import pallas as pl
from jax.experimental.pallas import tpu as pltpu

F32 = jnp.float32
BF16 = jnp.bfloat16
I32 = jnp.int32
U32 = jnp.uint32

D_MODEL = 1024
GRID_W = 64
ATT_HEADS = 8
ATT_KV_HEADS = 2
ATT_GROUP = ATT_HEADS // ATT_KV_HEADS
ATT_HEAD_DIM = 64
ATT_WIDTH = ATT_HEADS * ATT_HEAD_DIM
ATT_KV_WIDTH = ATT_KV_HEADS * ATT_HEAD_DIM
ROPE_THETA = 10000.0
HG_HEADS = 4
HG_DK = 128
HG_DV = 128
HG_WIDTH = HG_HEADS * HG_DK
HG_CHUNK = 64
HG_SUB = 16
N_BRANCH = 2
PEER_HEADS = 8
PEER_NKEYS = 128
PEER_DKEY = 256
PEER_TOPK = 16
PEER_PAIRS = PEER_HEADS * PEER_TOPK
PEER_N_EXPERTS = PEER_NKEYS * PEER_NKEYS
PLE_DIM = 256
EPS = 1e-6

LANES = 128
VMEM_LIMIT_BYTES = 56 * 1024 * 1024

TM_PROJ = 256
TQ_ATT = 256
TM_PEER = 128
PEER_TRIP = 8


def _cparams(n_axes):
    return pltpu.CompilerParams(
        dimension_semantics=("arbitrary",) * n_axes,
        vmem_limit_bytes=VMEM_LIMIT_BYTES,
    )


def _const_spec(shape):
    nd = len(shape)
    return pl.BlockSpec(shape, lambda *_: (0,) * nd)


def _dot(a, b):
    return jnp.dot(a, b, preferred_element_type=F32)


def _dot_nt(a, b):
    return lax.dot_general(a, b, (((1,), (1,)), ((), ())), preferred_element_type=F32)


def _dot_tn(a, b):
    return lax.dot_general(a, b, (((0,), (0,)), ((), ())), preferred_element_type=F32)


def _split3(x):
    hi = x.astype(BF16)
    r = x - hi.astype(F32)
    mid = r.astype(BF16)
    lo = (r - mid.astype(F32)).astype(BF16)
    return hi, mid, lo


def _rms(x, gain):
    ms = jnp.mean(x * x, axis=-1, keepdims=True)
    return x * lax.rsqrt(ms + EPS) * gain


def _head_rms_rope(a, gain, m_blk, cos, sin_signed, first_half):
    sq = a * a
    hi = sq.astype(BF16)
    lo = (sq - hi.astype(F32)).astype(BF16)
    ms = _dot(hi, m_blk) + _dot(lo, m_blk)
    y = a * lax.rsqrt(ms + EPS) * gain
    w = y.shape[-1]
    nxt = pltpu.roll(y, w - HG_SUB, axis=1)
    prv = pltpu.roll(y, HG_SUB, axis=1)
    partner = jnp.where(first_half, nxt, prv)
    return y * cos + partner * sin_signed


def _inproj_kernel(x_ref, nm_ref, wq_ref, wk_ref, wv_ref, whq_ref, wff_ref, wfb_ref, whi_ref,
                   whg_ref, wg_ref, qg_ref, kg_ref, cos_ref, sin_ref, m_ref, lbraw_ref,
                   q_out, k_out, v_out, hq_out, kf_out, lff_out, kb_out, lfb_out, hi_out,
                   hg_out, g_out):
    h = _rms(x_ref[...], nm_ref[...]).astype(BF16)

    def proj(w_ref):
        return _dot(h, w_ref[...])

    cos = cos_ref[...]
    sin = sin_ref[...]
    lane = lax.broadcasted_iota(I32, (1, LANES), 1)
    first_half = (lane % 32) < 16

    def tile_lanes(t, reps):
        return jnp.concatenate([t] * reps, axis=1) if reps > 1 else t

    rq = ATT_WIDTH // LANES
    q = _head_rms_rope(proj(wq_ref), qg_ref[...], m_ref[...], tile_lanes(cos, rq),
                       tile_lanes(sin, rq), tile_lanes(first_half, rq))
    q_out[...] = (q * (ATT_HEAD_DIM ** -0.5)).astype(BF16)
    k = _head_rms_rope(proj(wk_ref), kg_ref[...], m_ref[0:ATT_KV_WIDTH, 0:ATT_KV_WIDTH],
                       cos, sin, first_half)
    k_out[...] = k.astype(BF16)
    v_out[...] = proj(wv_ref).astype(BF16)

    hq = proj(whq_ref)
    hq_out[...] = hq * jax.nn.sigmoid(hq)

    r = lbraw_ref[...]
    for d, (w_ref, k_ref, lf_ref) in enumerate(((wff_ref, kf_out, lff_out),
                                                 (wfb_ref, kb_out, lfb_out))):
        a0 = r[d:d + 1]
        a1 = r[2 + d:3 + d]
        mx = jnp.maximum(a0, a1)
        e0 = jnp.exp(a0 - mx)
        e1 = jnp.exp(a1 - mx)
        lb = e0 / (e0 + e1)
        f = lb + (1.0 - lb) * jax.nn.sigmoid(proj(w_ref))
        k_ref[...] = 1.0 - f
        lf_ref[...] = jnp.log(f)

    hi_out[...] = proj(whi_ref)
    hg = proj(whg_ref)
    hg_out[...] = hg * jax.nn.sigmoid(hg)
    g_out[...] = jax.nn.sigmoid(proj(wg_ref))


def _rope_tables(seq_len):
    lane = np.arange(LANES)
    axis = (lane % ATT_HEAD_DIM) // 32
    j = lane % 16
    sign = np.where((lane % 32) < 16, -1.0, 1.0).astype(np.float32)
    rot_half = ATT_HEAD_DIM // 4
    inv = ROPE_THETA ** (-jnp.arange(rot_half, dtype=F32) / rot_half)
    t = jnp.arange(seq_len)
    pos = jnp.stack([(t // GRID_W).astype(F32), (t % GRID_W).astype(F32)], axis=1)
    ang = pos[:, axis] * inv[j][None, :]
    return jnp.cos(ang), jnp.sin(ang) * sign[None, :]


def _inproj(x2, seq_len, norm_mix, w_in, q_norm, k_norm, lb_raw):
    t_total = x2.shape[0]
    tm = TM_PROJ
    n_tiles = t_total // tm
    tiles_per_seq = seq_len // tm
    splits = (ATT_WIDTH, ATT_KV_WIDTH, ATT_KV_WIDTH, HG_WIDTH, HG_WIDTH, HG_WIDTH, HG_WIDTH,
              HG_WIDTH, N_BRANCH * D_MODEL)
    offs = np.cumsum((0,) + splits)
    w_bf = w_in.astype(BF16)
    ws = [w_bf[:, offs[i]:offs[i + 1]] for i in range(len(splits))]
    cos, sin = _rope_tables(seq_len)
    grp = np.arange(ATT_WIDTH) // ATT_HEAD_DIM
    m_blk = jnp.asarray((grp[:, None] == grp[None, :]).astype(np.float32) / ATT_HEAD_DIM, BF16)
    qg = jnp.tile(q_norm.astype(F32), ATT_HEADS)[None, :]
    kg = jnp.tile(k_norm.astype(F32), ATT_KV_HEADS)[None, :]

    def row_spec(width):
        return pl.BlockSpec((tm, width), lambda i: (i, 0))

    in_specs = [row_spec(D_MODEL), _const_spec((1, D_MODEL))]
    in_specs += [_const_spec((D_MODEL, s)) for s in splits]
    in_specs += [_const_spec((1, ATT_WIDTH)), _const_spec((1, ATT_KV_WIDTH)),
                 pl.BlockSpec((tm, LANES), lambda i: (i % tiles_per_seq, 0)),
                 pl.BlockSpec((tm, LANES), lambda i: (i % tiles_per_seq, 0)),
                 _const_spec((ATT_WIDTH, ATT_WIDTH)), _const_spec((4, HG_WIDTH))]
    out_widths = (ATT_WIDTH, ATT_KV_WIDTH, ATT_KV_WIDTH) + (HG_WIDTH,) * 7 + (N_BRANCH * D_MODEL,)
    out_dtypes = (BF16, BF16, BF16) + (F32,) * 8
    return pl.pallas_call(
        _inproj_kernel,
        grid=(n_tiles,),
        in_specs=in_specs,
        out_specs=[row_spec(w) for w in out_widths],
        out_shape=[jax.ShapeDtypeStruct((t_total, w), dt) for w, dt in zip(out_widths, out_dtypes)],
        compiler_params=_cparams(1),
        name="inproj",
    )(x2, norm_mix.reshape(1, D_MODEL), *ws, qg, kg, cos, sin, m_blk, lb_raw.reshape(4, HG_WIDTH))


def _attention_kernel(q_ref, k_ref, v_ref, o_ref):
    q = q_ref[...]
    k = k_ref[...]
    v = v_ref[...]
    outs = []
    for kvh in range(ATT_KV_HEADS):
        kh = k[:, kvh * ATT_HEAD_DIM:(kvh + 1) * ATT_HEAD_DIM]
        vh = v[:, kvh * ATT_HEAD_DIM:(kvh + 1) * ATT_HEAD_DIM]
        for g in range(ATT_GROUP):
            hd = kvh * ATT_GROUP + g
            qh = q[:, hd * ATT_HEAD_DIM:(hd + 1) * ATT_HEAD_DIM]
            s = _dot_nt(qh, kh)
            m = jnp.max(s, axis=-1, keepdims=True)
            p = jnp.exp(s - m)
            l = jnp.sum(p, axis=-1, keepdims=True)
            outs.append(_dot(p.astype(BF16), vh) / l)
    o_ref[...] = jnp.concatenate(outs, axis=1).astype(BF16)


def _attention(q, k, v, batch, seq_len):
    tq = TQ_ATT
    nq = seq_len // tq
    return pl.pallas_call(
        _attention_kernel,
        grid=(batch, nq),
        in_specs=[pl.BlockSpec((tq, ATT_WIDTH), lambda b, i: (b * nq + i, 0)),
                  pl.BlockSpec((seq_len, ATT_KV_WIDTH), lambda b, i: (b, 0)),
                  pl.BlockSpec((seq_len, ATT_KV_WIDTH), lambda b, i: (b, 0))],
        out_specs=pl.BlockSpec((tq, ATT_WIDTH), lambda b, i: (b * nq + i, 0)),
        out_shape=jax.ShapeDtypeStruct((batch * seq_len, ATT_WIDTH), BF16),
        compiler_params=_cparams(2),
        name="attention",
    )(q, k, v)


def _cum_log_decay(lf, reverse):
    c = lf.shape[0]
    row = lax.broadcasted_iota(I32, (c, c), 0)
    col = lax.broadcasted_iota(I32, (c, c), 1)
    tri = ((col >= row) if reverse else (col <= row)).astype(BF16)
    hi, mid, lo = _split3(lf)
    return _dot(tri, hi) + _dot(tri, mid) + _dot(tri, lo)


def _gla_chunk(q, k, v, b, st_ref, reverse):
    c = HG_CHUNK
    edge = 0 if reverse else c - 1
    b_all = b[edge:edge + 1]

    st = st_ref[...]
    o = _dot_nt((q * jnp.exp(b)).astype(BF16), st.astype(BF16))
    k_dec = k * jnp.exp(b_all - b)
    st_ref[...] = st * jnp.exp(b_all) + _dot_tn(v.astype(BF16), k_dec.astype(BF16))

    if reverse:
        blocks = ((0, 32, 32, 64, 32), (0, 16, 16, 32, 16), (32, 48, 48, 64, 48))
    else:
        blocks = ((32, 64, 0, 32, 31), (16, 32, 0, 16, 15), (48, 64, 32, 48, 47))
    pieces = {}
    for r0, r1, c0, c1, ref in blocks:
        b_ref = b[ref:ref + 1]
        qs = q[r0:r1] * jnp.exp(b[r0:r1] - b_ref)
        ks = k[c0:c1] * jnp.exp(b_ref - b[c0:c1])
        a = _dot_nt(qs.astype(BF16), ks.astype(BF16))
        pieces.setdefault(r0, []).append((r1, _dot(a.astype(BF16), v[c0:c1].astype(BF16))))

    out_rows = []
    sub = HG_SUB
    trow = lax.broadcasted_iota(I32, (sub, 1), 0)
    for i in range(c // sub):
        lo_r = i * sub
        qb = q[lo_r:lo_r + sub]
        bb = b[lo_r:lo_r + sub]
        acc = jnp.zeros((sub, HG_DV), F32)
        for s in range(sub):
            keep = (trow <= s) if reverse else (trow >= s)
            dec = jnp.exp(jnp.where(keep, bb - b[lo_r + s:lo_r + s + 1], -jnp.inf))
            a_col = jnp.sum(qb * k[lo_r + s:lo_r + s + 1] * dec, axis=1, keepdims=True)
            acc = acc + a_col * v[lo_r + s:lo_r + s + 1]
        out_rows.append(acc)
    intra = jnp.concatenate(out_rows, axis=0)
    for r0, plist in pieces.items():
        for r1, val in plist:
            pad_top = r0
            pad_bot = c - r1
            parts = []
            if pad_top:
                parts.append(jnp.zeros((pad_top, HG_DV), F32))
            parts.append(val)
            if pad_bot:
                parts.append(jnp.zeros((pad_bot, HG_DV), F32))
            intra = intra + jnp.concatenate(parts, axis=0)
    return o + intra


def _hgrn_kernel(qf_ref, kf_ref, lff_ref, vf_ref, qb_ref, kb_ref, lfb_ref, vb_ref,
                 of_ref, ob_ref, stf_ref, stb_ref):
    @pl.when(pl.program_id(1) == 0)
    def _():
        stf_ref[...] = jnp.zeros_like(stf_ref)
        stb_ref[...] = jnp.zeros_like(stb_ref)

    for refs, o_ref, st_ref, rev in (((qf_ref, kf_ref, vf_ref, lff_ref), of_ref, stf_ref, False),
                                     ((qb_ref, kb_ref, vb_ref, lfb_ref), ob_ref, stb_ref, True)):
        q_r, k_r, v_r, lf_r = refs
        b_heads = _cum_log_decay(lf_r[...], rev)
        outs = []
        for h in range(HG_HEADS):
            sl = slice(h * HG_DK, (h + 1) * HG_DK)
            outs.append(_gla_chunk(q_r[:, sl], k_r[:, sl], v_r[:, sl], b_heads[:, sl],
                                   st_ref.at[h], rev))
        o_ref[...] = jnp.concatenate(outs, axis=1)


def _hgrn(hq, kf, lff, kb, lfb, hi, batch, seq_len):
    c = HG_CHUNK
    nc = seq_len // c
    fwd = pl.BlockSpec((c, HG_WIDTH), lambda b, i: (b * nc + i, 0))
    bwd = pl.BlockSpec((c, HG_WIDTH), lambda b, i: (b * nc + (nc - 1 - i), 0))
    out = jax.ShapeDtypeStruct((batch * seq_len, HG_WIDTH), F32)
    return pl.pallas_call(
        _hgrn_kernel,
        grid=(batch, nc),
        in_specs=[fwd, fwd, fwd, fwd, bwd, bwd, bwd, bwd],
        out_specs=[fwd, bwd],
        out_shape=[out, out],
        scratch_shapes=[pltpu.VMEM((HG_HEADS, HG_DV, HG_DK), F32),
                        pltpu.VMEM((HG_HEADS, HG_DV, HG_DK), F32)],
        compiler_params=_cparams(2),
        name="hgrn_scan",
    )(hq, kf, lff, hi, hq, kb, lfb, hi)


def _merge_kernel(x_ref, att_ref, of_ref, ob_ref, hg_ref, g_ref, on_ref, wua_ref, wuh_ref,
                  wo_ref, nf_ref, x1_out, hf_out):
    o = of_ref[...] + ob_ref[...]
    gate = hg_ref[...]
    parts = []
    for h in range(HG_HEADS):
        sl = slice(h * HG_DV, (h + 1) * HG_DV)
        parts.append(_rms(o[:, sl], on_ref[...]) * gate[:, sl])
    hn = jnp.concatenate(parts, axis=1).astype(BF16)
    y_hg = _dot(hn, wuh_ref[...])
    y_att = _dot(att_ref[...], wua_ref[...])
    g = g_ref[...]
    merged = g[:, :D_MODEL] * y_att + g[:, D_MODEL:] * y_hg
    x1 = x_ref[...] + _dot(merged.astype(BF16), wo_ref[...])
    x1_out[...] = x1
    hf_out[...] = _rms(x1, nf_ref[...])


def _merge(x2, att, o_f, o_b, hg_s, g, out_norm, w_up_att, w_up_hg, w_out, norm_ffn):
    t_total = x2.shape[0]
    tm = TM_PROJ

    def row_spec(width):
        return pl.BlockSpec((tm, width), lambda i: (i, 0))

    out = jax.ShapeDtypeStruct((t_total, D_MODEL), F32)
    return pl.pallas_call(
        _merge_kernel,
        grid=(t_total // tm,),
        in_specs=[row_spec(D_MODEL), row_spec(ATT_WIDTH), row_spec(HG_WIDTH), row_spec(HG_WIDTH),
                  row_spec(HG_WIDTH), row_spec(N_BRANCH * D_MODEL), _const_spec((1, HG_DV)),
                  _const_spec((ATT_WIDTH, D_MODEL)), _const_spec((HG_WIDTH, D_MODEL)),
                  _const_spec((D_MODEL, D_MODEL)), _const_spec((1, D_MODEL))],
        out_specs=[row_spec(D_MODEL), row_spec(D_MODEL)],
        out_shape=[out, out],
        compiler_params=_cparams(1),
        name="merge",
    )(x2, att, o_f, o_b, hg_s, g, out_norm.reshape(1, HG_DV), w_up_att.astype(BF16),
      w_up_hg.astype(BF16), w_out.astype(BF16), norm_ffn.reshape(1, D_MODEL))


def _topk_rows(s, k, key):
    big = jnp.float32(2.0 ** 30)
    vals, picks = [], []
    for _ in range(k):
        m = jnp.max(s, axis=0, keepdims=True)
        kmin = jnp.min(jnp.where(s == m, key, big), axis=0, keepdims=True)
        vals.append(m)
        picks.append(kmin)
        s = jnp.where(key == kmin, -jnp.inf, s)
    return jnp.concatenate(vals, axis=0), jnp.concatenate(picks, axis=0)


def _peer_topk_kernel(hf_ref, wqt_ref, sk_ref, idx_out, g_out):
    qt = _dot_nt(wqt_ref[...], hf_ref[...].astype(BF16)).astype(BF16)
    kk = PEER_TOPK
    half = PEER_DKEY // 2
    tokens = hf_ref.shape[0]
    key_row = lax.broadcasted_iota(I32, (PEER_NKEYS, tokens), 0).astype(F32)
    row16 = key_row[:kk]
    ids = []
    for h in range(PEER_HEADS):
        tops = []
        for p in range(2):
            gi = h * 2 + p
            s = _dot(sk_ref[gi], qt[gi * half:(gi + 1) * half])
            tops.append(_topk_rows(s, kk, key_row))
        (v1, i1), (v2, i2) = tops
        vals, keys = [], []
        for a, nb in ((0, 16), (1, 8), (2, 8), (3, 8)):
            vals.append(v1[a:a + 1] + v2[:nb])
            keys.append((row16[:nb] + a * kk) * PEER_N_EXPERTS + (i1[a:a + 1] * PEER_NKEYS + i2[:nb]))
        for b, na in ((0, 16), (1, 8), (2, 8)):
            vals.append(jnp.where(row16[:na] >= 4, v1[:na] + v2[b:b + 1], -jnp.inf))
            keys.append((row16[:na] * kk + b) * PEER_N_EXPERTS + (i1[:na] * PEER_NKEYS + i2[b:b + 1]))
        top_s, key = _topk_rows(jnp.concatenate(vals, axis=0), kk, jnp.concatenate(keys, axis=0))
        idx = key - jnp.floor(key * (1.0 / PEER_N_EXPERTS)) * PEER_N_EXPERTS
        e = jnp.exp(top_s - jnp.max(top_s, axis=0, keepdims=True))
        g_out[h * kk:(h + 1) * kk, :] = e / jnp.sum(e, axis=0, keepdims=True)
        ids.append(idx.astype(I32))
    idx_out[...] = jnp.concatenate(ids, axis=0).T


def _peer_topk(hf, peer_wq, peer_subkeys):
    t_total = hf.shape[0]
    tm = TM_PEER
    wqt = peer_wq.T.astype(BF16)
    sk = peer_subkeys.reshape(PEER_HEADS * 2, PEER_NKEYS, PEER_DKEY // 2).astype(BF16)
    return pl.pallas_call(
        _peer_topk_kernel,
        grid=(t_total // tm,),
        in_specs=[pl.BlockSpec((tm, D_MODEL), lambda i: (i, 0)),
                  _const_spec((PEER_HEADS * PEER_DKEY, D_MODEL)),
                  _const_spec((PEER_HEADS * 2, PEER_NKEYS, PEER_DKEY // 2))],
        out_specs=[pl.BlockSpec((tm, PEER_PAIRS), lambda i: (i, 0)),
                   pl.BlockSpec((PEER_PAIRS, tm), lambda i: (0, i))],
        out_shape=[jax.ShapeDtypeStruct((t_total, PEER_PAIRS), I32),
                   jax.ShapeDtypeStruct((PEER_PAIRS, t_total), F32)],
        compiler_params=_cparams(1),
        name="peer_topk",
    )(hf, wqt, sk)


HALF_ROWS = 4
ROW_MASK_HI = 0xFFFF0000


def _pack_expert_rows(tab):
    e, d = tab.shape
    bits = lax.bitcast_convert_type(tab.astype(BF16), jnp.uint16).astype(U32)
    return (bits[:, :d // 2] | (bits[:, d // 2:] << 16)).reshape(e, HALF_ROWS, LANES)


def _two_rows(tab_ref, idx_ref, pa, pb, t):
    return jnp.concatenate([tab_ref[idx_ref[t, pa]], tab_ref[idx_ref[t, pb]]], axis=0)


def _unpack_lo(w):
    return lax.bitcast_convert_type(w << 16, F32)


def _unpack_hi(w):
    return lax.bitcast_convert_type(w & jnp.uint32(ROW_MASK_HI), F32)


def _row_halves(xrow):
    lo = [xrow[:, c * LANES:(c + 1) * LANES] for c in range(HALF_ROWS)]
    hi = [xrow[:, (HALF_ROWS + c) * LANES:(HALF_ROWS + c + 1) * LANES] for c in range(HALF_ROWS)]
    return jnp.concatenate(lo + lo, axis=0), jnp.concatenate(hi + hi, axis=0)


def _peer_act_kernel(idx_ref, g_ref, hf_ref, tab_ref, w_out):
    sub = lax.broadcasted_iota(I32, (2 * HALF_ROWS, LANES), 0)
    low2 = (sub % 4) < 2
    even = (sub % 2) < 1
    lane = lax.broadcasted_iota(I32, (PEER_PAIRS, TM_PEER), 1)

    def fold2(x, y):
        return jnp.where(low2, x + pltpu.roll(x, 6, axis=0), y + pltpu.roll(y, 2, axis=0))

    def fold1(x, y):
        return jnp.where(even, x + pltpu.roll(x, 7, axis=0), y + pltpu.roll(y, 1, axis=0))

    def dots(t):
        x_lo, x_hi = _row_halves(hf_ref[pl.ds(t, 1), :])

        def prod(pa, pb):
            w = _two_rows(tab_ref, idx_ref, pa, pb, t)
            return _unpack_lo(w) * x_lo + _unpack_hi(w) * x_hi

        a_parts = []
        for grp in range(PEER_PAIRS // 8):
            b = grp * 8
            r = fold1(fold2(prod(b, b + 4), prod(b + 2, b + 6)),
                      fold2(prod(b + 1, b + 5), prod(b + 3, b + 7)))
            a_parts.append(jnp.sum(r, axis=1, keepdims=True))
        return jnp.concatenate(a_parts, axis=0)

    def token_group(i, a_acc):
        t0 = PEER_TRIP * i
        cols = [dots(t0 + j) for j in range(PEER_TRIP)]
        for j in range(PEER_TRIP):
            a_acc = jnp.where(lane == t0 + j, cols[j], a_acc)
        return a_acc

    a_all = lax.fori_loop(0, TM_PEER // PEER_TRIP, token_group,
                          jnp.zeros((PEER_PAIRS, TM_PEER), F32))
    w_out[...] = jax.nn.gelu(a_all) * g_ref[...]


def _peer_act(idx_t, g_t, hf, peer_u):
    t_total = hf.shape[0]
    tm = TM_PEER
    return pl.pallas_call(
        _peer_act_kernel,
        grid=(t_total // tm,),
        in_specs=[pl.BlockSpec((tm, PEER_PAIRS), lambda i: (i, 0), memory_space=pltpu.SMEM),
                  pl.BlockSpec((PEER_PAIRS, tm), lambda i: (0, i)),
                  pl.BlockSpec((tm, D_MODEL), lambda i: (i, 0)),
                  pl.BlockSpec(memory_space=pltpu.VMEM)],
        out_specs=pl.BlockSpec((PEER_PAIRS, tm), lambda i: (0, i)),
        out_shape=jax.ShapeDtypeStruct((PEER_PAIRS, t_total), F32),
        compiler_params=_cparams(1),
        name="peer_act",
    )(idx_t, g_t, hf, _pack_expert_rows(peer_u))


def _peer_mix_kernel(idx_ref, w_ref, x1_ref, p_ref, np_ref, wg_ref, wp_ref, tab_ref, out_ref,
                     x2_scr):
    low4 = lax.broadcasted_iota(I32, (2 * HALF_ROWS, LANES), 0) < HALF_ROWS
    lane = lax.broadcasted_iota(I32, (PEER_PAIRS, TM_PEER), 1)
    tile = (2 * HALF_ROWS, LANES)

    def token(t):
        wcol = jnp.sum(jnp.where(lane == t, w_ref[...], 0.0), axis=1, keepdims=True)
        wcol = jnp.broadcast_to(wcol, (PEER_PAIRS, LANES))
        acc_lo = [jnp.zeros(tile, F32) for _ in range(2)]
        acc_hi = [jnp.zeros(tile, F32) for _ in range(2)]
        for q in range(PEER_PAIRS // 2):
            w = _two_rows(tab_ref, idx_ref, 2 * q, 2 * q + 1, t)
            wgt = jnp.where(low4, jnp.broadcast_to(wcol[2 * q:2 * q + 1], tile),
                            jnp.broadcast_to(wcol[2 * q + 1:2 * q + 2], tile))
            acc_lo[q % 2] = acc_lo[q % 2] + wgt * _unpack_lo(w)
            acc_hi[q % 2] = acc_hi[q % 2] + wgt * _unpack_hi(w)
        lo = acc_lo[0] + acc_lo[1]
        hi = acc_hi[0] + acc_hi[1]
        lo = lo[:HALF_ROWS] + lo[HALF_ROWS:]
        hi = hi[:HALF_ROWS] + hi[HALF_ROWS:]
        row = jnp.concatenate([lo[c:c + 1] for c in range(HALF_ROWS)]
                              + [hi[c:c + 1] for c in range(HALF_ROWS)], axis=1)
        x2_scr[pl.ds(t, 1), :] = x1_ref[pl.ds(t, 1), :] + row

    def token_group(i, carry):
        for j in range(PEER_TRIP):
            token(PEER_TRIP * i + j)
        return carry

    lax.fori_loop(0, TM_PEER // PEER_TRIP, token_group, 0)
    x2 = x2_scr[...]
    gate = jax.nn.sigmoid(_dot(_rms(x2, np_ref[...]).astype(BF16), wg_ref[...]))
    out_ref[...] = x2 + gate * _dot(p_ref[...].astype(BF16), wp_ref[...])


def _peer_mix_ple(idx_t, w_t, x1, peer_v, p2, norm_ple, ple_gate, ple_proj):
    t_total = x1.shape[0]
    tm = TM_PEER
    return pl.pallas_call(
        _peer_mix_kernel,
        grid=(t_total // tm,),
        in_specs=[pl.BlockSpec((tm, PEER_PAIRS), lambda i: (i, 0), memory_space=pltpu.SMEM),
                  pl.BlockSpec((PEER_PAIRS, tm), lambda i: (0, i)),
                  pl.BlockSpec((tm, D_MODEL), lambda i: (i, 0)),
                  pl.BlockSpec((tm, PLE_DIM), lambda i: (i, 0)),
                  _const_spec((1, D_MODEL)), _const_spec((D_MODEL, D_MODEL)),
                  _const_spec((PLE_DIM, D_MODEL)),
                  pl.BlockSpec(memory_space=pltpu.VMEM)],
        out_specs=pl.BlockSpec((tm, D_MODEL), lambda i: (i, 0)),
        out_shape=jax.ShapeDtypeStruct((t_total, D_MODEL), F32),
        scratch_shapes=[pltpu.VMEM((tm, D_MODEL), F32)],
        compiler_params=_cparams(1),
        name="peer_mix",
    )(idx_t, w_t, x1, p2, norm_ple.reshape(1, D_MODEL), ple_gate.astype(BF16),
      ple_proj.astype(BF16), _pack_expert_rows(peer_v))


def kernel(x, p, norm_mix, w_in, q_norm, k_norm, hg_lb_raw, hg_out_norm, w_up_att, w_up_hg, w_out,
           norm_ffn, peer_wq, peer_subkeys, peer_u, peer_v, norm_ple, ple_gate, ple_proj):
    batch, seq_len, d = x.shape
    depth = p.shape[0]
    assert d == D_MODEL and depth == 1 and hg_lb_raw.shape == (2, 2, HG_WIDTH)
    assert seq_len % TM_PROJ == 0 and seq_len % HG_CHUNK == 0 and (batch * seq_len) % TM_PEER == 0
    x2 = x.reshape(batch * seq_len, d)
    (q, k, v, hq, kf, lff, kb, lfb, hi, hg_s, g) = _inproj(
        x2, seq_len, norm_mix[0], w_in[0], q_norm[0], k_norm[0], hg_lb_raw)
    att = _attention(q, k, v, batch, seq_len)
    o_f, o_b = _hgrn(hq, kf, lff, kb, lfb, hi, batch, seq_len)
    x1, hf = _merge(x2, att, o_f, o_b, hg_s, g, hg_out_norm[0], w_up_att[0], w_up_hg[0], w_out[0],
                    norm_ffn[0])
    idx_t, g_t = _peer_topk(hf, peer_wq[0], peer_subkeys[0])
    w_t = _peer_act(idx_t, g_t, hf, peer_u[0])
    out = _peer_mix_ple(idx_t, w_t, x1, peer_v[0], p[0].reshape(batch * seq_len, PLE_DIM),
                        norm_ple[0], ple_gate[0], ple_proj[0])
    return out.reshape(batch, seq_len, d)
```

```python
import functools
import math

import jax
import jax.numpy as jnp
import numpy as np
from jax import lax
from jax.experimental import pallas as pl
from jax.experimental.pallas import tpu as pltpu

F32 = jnp.float32
BF16 = jnp.bfloat16
I32 = jnp.int32
U32 = jnp.uint32

D_MODEL = 1024
GRID_W = 64
ATT_HEADS = 8
ATT_KV_HEADS = 2
ATT_GROUP = ATT_HEADS // ATT_KV_HEADS
ATT_HEAD_DIM = 64
ATT_WIDTH = ATT_HEADS * ATT_HEAD_DIM
ATT_KV_WIDTH = ATT_KV_HEADS * ATT_HEAD_DIM
ROPE_THETA = 10000.0
HG_HEADS = 4
HG_DK = 128
HG_DV = 128
HG_WIDTH = HG_HEADS * HG_DK
HG_CHUNK = 64
HG_SUB = 16
N_BRANCH = 2
PEER_HEADS = 8
PEER_NKEYS = 128
PEER_DKEY = 256
PEER_TOPK = 16
PEER_PAIRS = PEER_HEADS * PEER_TOPK
PEER_N_EXPERTS = PEER_NKEYS * PEER_NKEYS
PLE_DIM = 256
EPS = 1e-6

LANES = 128
VMEM_LIMIT_BYTES = 56 * 1024 * 1024

TM_PROJ = 256
TQ_ATT = 512
TM_PEER = 128
PEER_TRIP = 8


def _cparams(n_axes):
    return pltpu.CompilerParams(
        dimension_semantics=("arbitrary",) * n_axes,
        vmem_limit_bytes=VMEM_LIMIT_BYTES,
    )


def _const_spec(shape):
    nd = len(shape)
    return pl.BlockSpec(shape, lambda *_: (0,) * nd)


def _dot(a, b):
    return jnp.dot(a, b, preferred_element_type=F32)


def _dot_nt(a, b):
    return lax.dot_general(a, b, (((1,), (1,)), ((), ())), preferred_element_type=F32)


def _dot_tn(a, b):
    return lax.dot_general(a, b, (((0,), (0,)), ((), ())), preferred_element_type=F32)


def _split3(x):
    hi = x.astype(BF16)
    r = x - hi.astype(F32)
    mid = r.astype(BF16)
    lo = (r - mid.astype(F32)).astype(BF16)
    return hi, mid, lo


def _rms(x, gain):
    ms = jnp.mean(x * x, axis=-1, keepdims=True)
    return x * lax.rsqrt(ms + EPS) * gain


def _head_rms_rope(a, gain, m_blk, cos, sin_signed, first_half):
    sq = a * a
    hi = sq.astype(BF16)
    lo = (sq - hi.astype(F32)).astype(BF16)
    ms = _dot(hi, m_blk) + _dot(lo, m_blk)
    y = a * lax.rsqrt(ms + EPS) * gain
    w = y.shape[-1]
    nxt = pltpu.roll(y, w - HG_SUB, axis=1)
    prv = pltpu.roll(y, HG_SUB, axis=1)
    partner = jnp.where(first_half, nxt, prv)
    return y * cos + partner * sin_signed


def _inproj_kernel(x_ref, nm_ref, wq_ref, wk_ref, wv_ref, whq_ref, wff_ref, wfb_ref, whi_ref,
                   whg_ref, wg_ref, qg_ref, kg_ref, cos_ref, sin_ref, m_ref, lbraw_ref,
                   q_out, k_out, v_out, hq_out, kf_out, lff_out, kb_out, lfb_out, hi_out,
                   hg_out, g_out):
    h = _rms(x_ref[...], nm_ref[...]).astype(BF16)

    def proj(w_ref):
        return _dot(h, w_ref[...])

    cos = cos_ref[...]
    sin = sin_ref[...]
    lane = lax.broadcasted_iota(I32, (1, LANES), 1)
    first_half = (lane % 32) < 16

    def tile_lanes(t, reps):
        return jnp.concatenate([t] * reps, axis=1) if reps > 1 else t

    rq = ATT_WIDTH // LANES
    q = _head_rms_rope(proj(wq_ref), qg_ref[...], m_ref[...], tile_lanes(cos, rq),
                       tile_lanes(sin, rq), tile_lanes(first_half, rq))
    q_out[...] = (q * (ATT_HEAD_DIM ** -0.5)).astype(BF16)
    k = _head_rms_rope(proj(wk_ref), kg_ref[...], m_ref[0:ATT_KV_WIDTH, 0:ATT_KV_WIDTH],
                       cos, sin, first_half)
    k_out[...] = k.astype(BF16)
    v_out[...] = proj(wv_ref).astype(BF16)

    hq = proj(whq_ref)
    hq_out[...] = hq * jax.nn.sigmoid(hq)

    r = lbraw_ref[...]
    for d, (w_ref, k_ref, lf_ref) in enumerate(((wff_ref, kf_out, lff_out),
                                                 (wfb_ref, kb_out, lfb_out))):
        a0 = r[d:d + 1]
        a1 = r[2 + d:3 + d]
        mx = jnp.maximum(a0, a1)
        e0 = jnp.exp(a0 - mx)
        e1 = jnp.exp(a1 - mx)
        lb = e0 / (e0 + e1)
        f = lb + (1.0 - lb) * jax.nn.sigmoid(proj(w_ref))
        k_ref[...] = 1.0 - f
        lf_ref[...] = jnp.log(f)

    hi_out[...] = proj(whi_ref)
    hg = proj(whg_ref)
    hg_out[...] = hg * jax.nn.sigmoid(hg)
    g_out[...] = jax.nn.sigmoid(proj(wg_ref))


def _rope_tables(seq_len):
    lane = np.arange(LANES)
    axis = (lane % ATT_HEAD_DIM) // 32
    j = lane % 16
    sign = np.where((lane % 32) < 16, -1.0, 1.0).astype(np.float32)
    rot_half = ATT_HEAD_DIM // 4
    inv = ROPE_THETA ** (-jnp.arange(rot_half, dtype=F32) / rot_half)
    t = jnp.arange(seq_len)
    pos = jnp.stack([(t // GRID_W).astype(F32), (t % GRID_W).astype(F32)], axis=1)
    ang = pos[:, axis] * inv[j][None, :]
    return jnp.cos(ang), jnp.sin(ang) * sign[None, :]


def _inproj(x2, seq_len, norm_mix, w_in, q_norm, k_norm, lb_raw):
    t_total = x2.shape[0]
    tm = TM_PROJ
    n_tiles = t_total // tm
    tiles_per_seq = seq_len // tm
    splits = (ATT_WIDTH, ATT_KV_WIDTH, ATT_KV_WIDTH, HG_WIDTH, HG_WIDTH, HG_WIDTH, HG_WIDTH,
              HG_WIDTH, N_BRANCH * D_MODEL)
    offs = np.cumsum((0,) + splits)
    w_bf = w_in.astype(BF16)
    ws = [w_bf[:, offs[i]:offs[i + 1]] for i in range(len(splits))]
    cos, sin = _rope_tables(seq_len)
    grp = np.arange(ATT_WIDTH) // ATT_HEAD_DIM
    m_blk = jnp.asarray((grp[:, None] == grp[None, :]).astype(np.float32) / ATT_HEAD_DIM, BF16)
    qg = jnp.tile(q_norm.astype(F32), ATT_HEADS)[None, :]
    kg = jnp.tile(k_norm.astype(F32), ATT_KV_HEADS)[None, :]

    def row_spec(width):
        return pl.BlockSpec((tm, width), lambda i: (i, 0))

    in_specs = [row_spec(D_MODEL), _const_spec((1, D_MODEL))]
    in_specs += [_const_spec((D_MODEL, s)) for s in splits]
    in_specs += [_const_spec((1, ATT_WIDTH)), _const_spec((1, ATT_KV_WIDTH)),
                 pl.BlockSpec((tm, LANES), lambda i: (i % tiles_per_seq, 0)),
                 pl.BlockSpec((tm, LANES), lambda i: (i % tiles_per_seq, 0)),
                 _const_spec((ATT_WIDTH, ATT_WIDTH)), _const_spec((4, HG_WIDTH))]
    out_widths = (ATT_WIDTH, ATT_KV_WIDTH, ATT_KV_WIDTH) + (HG_WIDTH,) * 7 + (N_BRANCH * D_MODEL,)
    out_dtypes = (BF16, BF16, BF16) + (F32,) * 8
    return pl.pallas_call(
        _inproj_kernel,
        grid=(n_tiles,),
        in_specs=in_specs,
        out_specs=[row_spec(w) for w in out_widths],
        out_shape=[jax.ShapeDtypeStruct((t_total, w), dt) for w, dt in zip(out_widths, out_dtypes)],
        compiler_params=_cparams(1),
        name="inproj",
    )(x2, norm_mix.reshape(1, D_MODEL), *ws, qg, kg, cos, sin, m_blk, lb_raw.reshape(4, HG_WIDTH))


def _attention_kernel(q_ref, k_ref, v_ref, o_ref):
    q = q_ref[...]
    k = k_ref[...]
    v = v_ref[...]
    outs = []
    for kvh in range(ATT_KV_HEADS):
        kh = k[:, kvh * ATT_HEAD_DIM:(kvh + 1) * ATT_HEAD_DIM]
        vh = v[:, kvh * ATT_HEAD_DIM:(kvh + 1) * ATT_HEAD_DIM]
        for g in range(ATT_GROUP):
            hd = kvh * ATT_GROUP + g
            qh = q[:, hd * ATT_HEAD_DIM:(hd + 1) * ATT_HEAD_DIM]
            s = _dot_nt(qh, kh)
            m = jnp.max(s, axis=-1, keepdims=True)
            p = jnp.exp(s - m)
            l = jnp.sum(p, axis=-1, keepdims=True)
            outs.append(_dot(p.astype(BF16), vh) / l)
    o_ref[...] = jnp.concatenate(outs, axis=1).astype(BF16)


def _attention(q, k, v, batch, seq_len):
    tq = TQ_ATT
    nq = seq_len // tq
    return pl.pallas_call(
        _attention_kernel,
        grid=(batch, nq),
        in_specs=[pl.BlockSpec((tq, ATT_WIDTH), lambda b, i: (b * nq + i, 0)),
                  pl.BlockSpec((seq_len, ATT_KV_WIDTH), lambda b, i: (b, 0)),
                  pl.BlockSpec((seq_len, ATT_KV_WIDTH), lambda b, i: (b, 0))],
        out_specs=pl.BlockSpec((tq, ATT_WIDTH), lambda b, i: (b * nq + i, 0)),
        out_shape=jax.ShapeDtypeStruct((batch * seq_len, ATT_WIDTH), BF16),
        compiler_params=_cparams(2),
        name="attention",
    )(q, k, v)


def _cum_log_decay(lf, reverse):
    c = lf.shape[0]
    row = lax.broadcasted_iota(I32, (c, c), 0)
    col = lax.broadcasted_iota(I32, (c, c), 1)
    tri = ((col >= row) if reverse else (col <= row)).astype(BF16)
    hi, mid, lo = _split3(lf)
    return _dot(tri, hi) + _dot(tri, mid) + _dot(tri, lo)


def _gla_chunk(q, k, v, b, st_ref, reverse):
    c = HG_CHUNK
    edge = 0 if reverse else c - 1
    b_all = b[edge:edge + 1]

    st = st_ref[...]
    o = _dot_nt((q * jnp.exp(b)).astype(BF16), st.astype(BF16))
    k_dec = k * jnp.exp(b_all - b)
    st_ref[...] = st * jnp.exp(b_all) + _dot_tn(v.astype(BF16), k_dec.astype(BF16))

    if reverse:
        blocks = ((0, 32, 32, 64, 32), (0, 16, 16, 32, 16), (32, 48, 48, 64, 48))
    else:
        blocks = ((32, 64, 0, 32, 31), (16, 32, 0, 16, 15), (48, 64, 32, 48, 47))
    pieces = {}
    for r0, r1, c0, c1, ref in blocks:
        b_ref = b[ref:ref + 1]
        qs = q[r0:r1] * jnp.exp(b[r0:r1] - b_ref)
        ks = k[c0:c1] * jnp.exp(b_ref - b[c0:c1])
        a = _dot_nt(qs.astype(BF16), ks.astype(BF16))
        pieces.setdefault(r0, []).append((r1, _dot(a.astype(BF16), v[c0:c1].astype(BF16))))

    out_rows = []
    sub = HG_SUB
    trow = lax.broadcasted_iota(I32, (sub, 1), 0)
    for i in range(c // sub):
        lo_r = i * sub
        qb = q[lo_r:lo_r + sub]
        bb = b[lo_r:lo_r + sub]
        acc = jnp.zeros((sub, HG_DV), F32)
        for s in range(sub):
            keep = (trow <= s) if reverse else (trow >= s)
            dec = jnp.exp(jnp.where(keep, bb - b[lo_r + s:lo_r + s + 1], -jnp.inf))
            a_col = jnp.sum(qb * k[lo_r + s:lo_r + s + 1] * dec, axis=1, keepdims=True)
            acc = acc + a_col * v[lo_r + s:lo_r + s + 1]
        out_rows.append(acc)
    intra = jnp.concatenate(out_rows, axis=0)
    for r0, plist in pieces.items():
        for r1, val in plist:
            pad_top = r0
            pad_bot = c - r1
            parts = []
            if pad_top:
                parts.append(jnp.zeros((pad_top, HG_DV), F32))
            parts.append(val)
            if pad_bot:
                parts.append(jnp.zeros((pad_bot, HG_DV), F32))
            intra = intra + jnp.concatenate(parts, axis=0)
    return o + intra


def _hgrn_kernel(qf_ref, kf_ref, lff_ref, vf_ref, qb_ref, kb_ref, lfb_ref, vb_ref,
                 of_ref, ob_ref, stf_ref, stb_ref):
    @pl.when(pl.program_id(1) == 0)
    def _():
        stf_ref[...] = jnp.zeros_like(stf_ref)
        stb_ref[...] = jnp.zeros_like(stb_ref)

    for refs, o_ref, st_ref, rev in (((qf_ref, kf_ref, vf_ref, lff_ref), of_ref, stf_ref, False),
                                     ((qb_ref, kb_ref, vb_ref, lfb_ref), ob_ref, stb_ref, True)):
        q_r, k_r, v_r, lf_r = refs
        b_heads = _cum_log_decay(lf_r[...], rev)
        outs = []
        for h in range(HG_HEADS):
            sl = slice(h * HG_DK, (h + 1) * HG_DK)
            outs.append(_gla_chunk(q_r[:, sl], k_r[:, sl], v_r[:, sl], b_heads[:, sl],
                                   st_ref.at[h], rev))
        o_ref[...] = jnp.concatenate(outs, axis=1)


def _hgrn(hq, kf, lff, kb, lfb, hi, batch, seq_len):
    c = HG_CHUNK
    nc = seq_len // c
    fwd = pl.BlockSpec((c, HG_WIDTH), lambda b, i: (b * nc + i, 0))
    bwd = pl.BlockSpec((c, HG_WIDTH), lambda b, i: (b * nc + (nc - 1 - i), 0))
    out = jax.ShapeDtypeStruct((batch * seq_len, HG_WIDTH), F32)
    return pl.pallas_call(
        _hgrn_kernel,
        grid=(batch, nc),
        in_specs=[fwd, fwd, fwd, fwd, bwd, bwd, bwd, bwd],
        out_specs=[fwd, bwd],
        out_shape=[out, out],
        scratch_shapes=[pltpu.VMEM((HG_HEADS, HG_DV, HG_DK), F32),
                        pltpu.VMEM((HG_HEADS, HG_DV, HG_DK), F32)],
        compiler_params=_cparams(2),
        name="hgrn_scan",
    )(hq, kf, lff, hi, hq, kb, lfb, hi)


def _merge_kernel(x_ref, att_ref, of_ref, ob_ref, hg_ref, g_ref, on_ref, wua_ref, wuh_ref,
                  wo_ref, nf_ref, x1_out, hf_out):
    o = of_ref[...] + ob_ref[...]
    gate = hg_ref[...]
    parts = []
    for h in range(HG_HEADS):
        sl = slice(h * HG_DV, (h + 1) * HG_DV)
        parts.append(_rms(o[:, sl], on_ref[...]) * gate[:, sl])
    hn = jnp.concatenate(parts, axis=1).astype(BF16)
    y_hg = _dot(hn, wuh_ref[...])
    y_att = _dot(att_ref[...], wua_ref[...])
    g = g_ref[...]
    merged = g[:, :D_MODEL] * y_att + g[:, D_MODEL:] * y_hg
    x1 = x_ref[...] + _dot(merged.astype(BF16), wo_ref[...])
    x1_out[...] = x1
    hf_out[...] = _rms(x1, nf_ref[...])


def _merge(x2, att, o_f, o_b, hg_s, g, out_norm, w_up_att, w_up_hg, w_out, norm_ffn):
    t_total = x2.shape[0]
    tm = TM_PROJ

    def row_spec(width):
        return pl.BlockSpec((tm, width), lambda i: (i, 0))

    out = jax.ShapeDtypeStruct((t_total, D_MODEL), F32)
    return pl.pallas_call(
        _merge_kernel,
        grid=(t_total // tm,),
        in_specs=[row_spec(D_MODEL), row_spec(ATT_WIDTH), row_spec(HG_WIDTH), row_spec(HG_WIDTH),
                  row_spec(HG_WIDTH), row_spec(N_BRANCH * D_MODEL), _const_spec((1, HG_DV)),
                  _const_spec((ATT_WIDTH, D_MODEL)), _const_spec((HG_WIDTH, D_MODEL)),
                  _const_spec((D_MODEL, D_MODEL)), _const_spec((1, D_MODEL))],
        out_specs=[row_spec(D_MODEL), row_spec(D_MODEL)],
        out_shape=[out, out],
        compiler_params=_cparams(1),
        name="merge",
    )(x2, att, o_f, o_b, hg_s, g, out_norm.reshape(1, HG_DV), w_up_att.astype(BF16),
      w_up_hg.astype(BF16), w_out.astype(BF16), norm_ffn.reshape(1, D_MODEL))


def _topk_rows(s, k, key):
    big = jnp.float32(2.0 ** 30)
    vals, picks = [], []
    for _ in range(k):
        m = jnp.max(s, axis=0, keepdims=True)
        kmin = jnp.min(jnp.where(s == m, key, big), axis=0, keepdims=True)
        vals.append(m)
        picks.append(kmin)
        s = jnp.where(key == kmin, -jnp.inf, s)
    return jnp.concatenate(vals, axis=0), jnp.concatenate(picks, axis=0)


def _peer_topk_kernel(hf_ref, wqt_ref, sk_ref, idx_out, g_out):
    qt = _dot_nt(wqt_ref[...], hf_ref[...].astype(BF16)).astype(BF16)
    kk = PEER_TOPK
    half = PEER_DKEY // 2
    tokens = hf_ref.shape[0]
    key_row = lax.broadcasted_iota(I32, (PEER_NKEYS, tokens), 0).astype(F32)
    row16 = key_row[:kk]
    ids = []
    for h in range(PEER_HEADS):
        tops = []
        for p in range(2):
            gi = h * 2 + p
            s = _dot(sk_ref[gi], qt[gi * half:(gi + 1) * half])
            tops.append(_topk_rows(s, kk, key_row))
        (v1, i1), (v2, i2) = tops
        vals, keys = [], []
        for a, nb in ((0, 16), (1, 8), (2, 8), (3, 8)):
            vals.append(v1[a:a + 1] + v2[:nb])
            keys.append((row16[:nb] + a * kk) * PEER_N_EXPERTS + (i1[a:a + 1] * PEER_NKEYS + i2[:nb]))
        for b, na in ((0, 16), (1, 8), (2, 8)):
            vals.append(jnp.where(row16[:na] >= 4, v1[:na] + v2[b:b + 1], -jnp.inf))
            keys.append((row16[:na] * kk + b) * PEER_N_EXPERTS + (i1[:na] * PEER_NKEYS + i2[b:b + 1]))
        top_s, key = _topk_rows(jnp.concatenate(vals, axis=0), kk, jnp.concatenate(keys, axis=0))
        idx = key - jnp.floor(key * (1.0 / PEER_N_EXPERTS)) * PEER_N_EXPERTS
        e = jnp.exp(top_s - jnp.max(top_s, axis=0, keepdims=True))
        g_out[h * kk:(h + 1) * kk, :] = e / jnp.sum(e, axis=0, keepdims=True)
        ids.append(idx.astype(I32))
    idx_out[...] = jnp.concatenate(ids, axis=0).T


def _peer_topk(hf, peer_wq, peer_subkeys):
    t_total = hf.shape[0]
    tm = TM_PEER
    wqt = peer_wq.T.astype(BF16)
    sk = peer_subkeys.reshape(PEER_HEADS * 2, PEER_NKEYS, PEER_DKEY // 2).astype(BF16)
    return pl.pallas_call(
        _peer_topk_kernel,
        grid=(t_total // tm,),
        in_specs=[pl.BlockSpec((tm, D_MODEL), lambda i: (i, 0)),
                  _const_spec((PEER_HEADS * PEER_DKEY, D_MODEL)),
                  _const_spec((PEER_HEADS * 2, PEER_NKEYS, PEER_DKEY // 2))],
        out_specs=[pl.BlockSpec((tm, PEER_PAIRS), lambda i: (i, 0)),
                   pl.BlockSpec((PEER_PAIRS, tm), lambda i: (0, i))],
        out_shape=[jax.ShapeDtypeStruct((t_total, PEER_PAIRS), I32),
                   jax.ShapeDtypeStruct((PEER_PAIRS, t_total), F32)],
        compiler_params=_cparams(1),
        name="peer_topk",
    )(hf, wqt, sk)


HALF_ROWS = 4
ROW_MASK_HI = 0xFFFF0000


def _pack_expert_rows(tab):
    e, d = tab.shape
    bits = lax.bitcast_convert_type(tab.astype(BF16), jnp.uint16).astype(U32)
    return (bits[:, :d // 2] | (bits[:, d // 2:] << 16)).reshape(e, HALF_ROWS, LANES)


def _two_rows(tab_ref, idx_ref, pa, pb, t):
    return jnp.concatenate([tab_ref[idx_ref[t, pa]], tab_ref[idx_ref[t, pb]]], axis=0)


def _unpack_lo(w):
    return lax.bitcast_convert_type(w << 16, F32)


def _unpack_hi(w):
    return lax.bitcast_convert_type(w & jnp.uint32(ROW_MASK_HI), F32)


def _row_halves(xrow):
    lo = [xrow[:, c * LANES:(c + 1) * LANES] for c in range(HALF_ROWS)]
    hi = [xrow[:, (HALF_ROWS + c) * LANES:(HALF_ROWS + c + 1) * LANES] for c in range(HALF_ROWS)]
    return jnp.concatenate(lo + lo, axis=0), jnp.concatenate(hi + hi, axis=0)


def _peer_act_kernel(idx_ref, g_ref, hf_ref, tab_ref, w_out):
    sub = lax.broadcasted_iota(I32, (2 * HALF_ROWS, LANES), 0)
    low2 = (sub % 4) < 2
    even = (sub % 2) < 1
    lane = lax.broadcasted_iota(I32, (PEER_PAIRS, TM_PEER), 1)

    def fold2(x, y):
        return jnp.where(low2, x + pltpu.roll(x, 6, axis=0), y + pltpu.roll(y, 2, axis=0))

    def fold1(x, y):
        return jnp.where(even, x + pltpu.roll(x, 7, axis=0), y + pltpu.roll(y, 1, axis=0))

    def dots(t):
        x_lo, x_hi = _row_halves(hf_ref[pl.ds(t, 1), :])

        def prod(pa, pb):
            w = _two_rows(tab_ref, idx_ref, pa, pb, t)
            return _unpack_lo(w) * x_lo + _unpack_hi(w) * x_hi

        a_parts = []
        for grp in range(PEER_PAIRS // 8):
            b = grp * 8
            r = fold1(fold2(prod(b, b + 4), prod(b + 2, b + 6)),
                      fold2(prod(b + 1, b + 5), prod(b + 3, b + 7)))
            a_parts.append(jnp.sum(r, axis=1, keepdims=True))
        return jnp.concatenate(a_parts, axis=0)

    def token_group(i, a_acc):
        t0 = PEER_TRIP * i
        cols = [dots(t0 + j) for j in range(PEER_TRIP)]
        for j in range(PEER_TRIP):
            a_acc = jnp.where(lane == t0 + j, cols[j], a_acc)
        return a_acc

    a_all = lax.fori_loop(0, TM_PEER // PEER_TRIP, token_group,
                          jnp.zeros((PEER_PAIRS, TM_PEER), F32))
    w_out[...] = jax.nn.gelu(a_all) * g_ref[...]


def _peer_act(idx_t, g_t, hf, peer_u):
    t_total = hf.shape[0]
    tm = TM_PEER
    return pl.pallas_call(
        _peer_act_kernel,
        grid=(t_total // tm,),
        in_specs=[pl.BlockSpec((tm, PEER_PAIRS), lambda i: (i, 0), memory_space=pltpu.SMEM),
                  pl.BlockSpec((PEER_PAIRS, tm), lambda i: (0, i)),
                  pl.BlockSpec((tm, D_MODEL), lambda i: (i, 0)),
                  pl.BlockSpec(memory_space=pltpu.VMEM)],
        out_specs=pl.BlockSpec((PEER_PAIRS, tm), lambda i: (0, i)),
        out_shape=jax.ShapeDtypeStruct((PEER_PAIRS, t_total), F32),
        compiler_params=_cparams(1),
        name="peer_act",
    )(idx_t, g_t, hf, _pack_expert_rows(peer_u))


def _peer_mix_kernel(idx_ref, w_ref, x1_ref, p_ref, np_ref, wg_ref, wp_ref, tab_ref, out_ref,
                     x2_scr):
    low4 = lax.broadcasted_iota(I32, (2 * HALF_ROWS, LANES), 0) < HALF_ROWS
    lane = lax.broadcasted_iota(I32, (PEER_PAIRS, TM_PEER), 1)
    tile = (2 * HALF_ROWS, LANES)

    def token(t):
        wcol = jnp.sum(jnp.where(lane == t, w_ref[...], 0.0), axis=1, keepdims=True)
        wcol = jnp.broadcast_to(wcol, (PEER_PAIRS, LANES))
        acc_lo = [jnp.zeros(tile, F32) for _ in range(2)]
        acc_hi = [jnp.zeros(tile, F32) for _ in range(2)]
        for q in range(PEER_PAIRS // 2):
            w = _two_rows(tab_ref, idx_ref, 2 * q, 2 * q + 1, t)
            wgt = jnp.where(low4, jnp.broadcast_to(wcol[2 * q:2 * q + 1], tile),
                            jnp.broadcast_to(wcol[2 * q + 1:2 * q + 2], tile))
            acc_lo[q % 2] = acc_lo[q % 2] + wgt * _unpack_lo(w)
            acc_hi[q % 2] = acc_hi[q % 2] + wgt * _unpack_hi(w)
        lo = acc_lo[0] + acc_lo[1]
        hi = acc_hi[0] + acc_hi[1]
        lo = lo[:HALF_ROWS] + lo[HALF_ROWS:]
        hi = hi[:HALF_ROWS] + hi[HALF_ROWS:]
        row = jnp.concatenate([lo[c:c + 1] for c in range(HALF_ROWS)]
                              + [hi[c:c + 1] for c in range(HALF_ROWS)], axis=1)
        x2_scr[pl.ds(t, 1), :] = x1_ref[pl.ds(t, 1), :] + row

    def token_group(i, carry):
        for j in range(PEER_TRIP):
            token(PEER_TRIP * i + j)
        return carry

    lax.fori_loop(0, TM_PEER // PEER_TRIP, token_group, 0)
    x2 = x2_scr[...]
    gate = jax.nn.sigmoid(_dot(_rms(x2, np_ref[...]).astype(BF16), wg_ref[...]))
    out_ref[...] = x2 + gate * _dot(p_ref[...].astype(BF16), wp_ref[...])


def _peer_mix_ple(idx_t, w_t, x1, peer_v, p2, norm_ple, ple_gate, ple_proj):
    t_total = x1.shape[0]
    tm = TM_PEER
    return pl.pallas_call(
        _peer_mix_kernel,
        grid=(t_total // tm,),
        in_specs=[pl.BlockSpec((tm, PEER_PAIRS), lambda i: (i, 0), memory_space=pltpu.SMEM),
                  pl.BlockSpec((PEER_PAIRS, tm), lambda i: (0, i)),
                  pl.BlockSpec((tm, D_MODEL), lambda i: (i, 0)),
                  pl.BlockSpec((tm, PLE_DIM), lambda i: (i, 0)),
                  _const_spec((1, D_MODEL)), _const_spec((D_MODEL, D_MODEL)),
                  _const_spec((PLE_DIM, D_MODEL)),
                  pl.BlockSpec(memory_space=pltpu.VMEM)],
        out_specs=pl.BlockSpec((tm, D_MODEL), lambda i: (i, 0)),
        out_shape=jax.ShapeDtypeStruct((t_total, D_MODEL), F32),
        scratch_shapes=[pltpu.VMEM((tm, D_MODEL), F32)],
        compiler_params=_cparams(1),
        name="peer_mix",
    )(idx_t, w_t, x1, p2, norm_ple.reshape(1, D_MODEL), ple_gate.astype(BF16),
      ple_proj.astype(BF16), _pack_expert_rows(peer_v))


def kernel(x, p, norm_mix, w_in, q_norm, k_norm, hg_lb_raw, hg_out_norm, w_up_att, w_up_hg, w_out,
           norm_ffn, peer_wq, peer_subkeys, peer_u, peer_v, norm_ple, ple_gate, ple_proj):
    batch, seq_len, d = x.shape
    depth = p.shape[0]
    assert d == D_MODEL and depth == 1 and hg_lb_raw.shape == (2, 2, HG_WIDTH)
    assert seq_len % TM_PROJ == 0 and seq_len % HG_CHUNK == 0 and (batch * seq_len) % TM_PEER == 0
    x2 = x.reshape(batch * seq_len, d)
    (q, k, v, hq, kf, lff, kb, lfb, hi, hg_s, g) = _inproj(
        x2, seq_len, norm_mix[0], w_in[0], q_norm[0], k_norm[0], hg_lb_raw)
    att = _attention(q, k, v, batch, seq_len)
    o_f, o_b = _hgrn(hq, kf, lff, kb, lfb, hi, batch, seq_len)
    x1, hf = _merge(x2, att, o_f, o_b, hg_s, g, hg_out_norm[0], w_up_att[0], w_up_hg[0], w_out[0],
                    norm_ffn[0])
    idx_t, g_t = _peer_topk(hf, peer_wq[0], peer_subkeys[0])
    w_t = _peer_act(idx_t, g_t, hf, peer_u[0])
    out = _peer_mix_ple(idx_t, w_t, x1, peer_v[0], p[0].reshape(batch * seq_len, PLE_DIM),
                        norm_ple[0], ple_gate[0], ple_proj[0])
    return out.reshape(batch, seq_len, d)
```
